```python
import jax, jax.numpy as jnp
from jax import lax
import numpy as np

D_MODEL = 1024
BATCH = 8
SEQ = 4096
DEPTH = 1
DEC_BATCH = 1
DEC_SEQ = 16384
PAST_LEN = 128

GRID_W = 64
GLA_HEADS = 4
GLA_DK = 64
GLA_DV = 128
GLA_LOWRANK = 16
GLA_TAU = 16.0
GLA_CHUNK = 64
NA_HEADS = 8
NA_DH = 64
NA_WIN_R = 8
NA_WIN_C = 16
NA_QBLOCK_C = 16
NA_BAND_C = 32
MIX_W = GLA_HEADS * GLA_DV + NA_HEADS * NA_DH
D_FF = 2816
CONV_W = 3
LN_EPS = 1e-5
RMS_EPS = 1e-6
DEEPNORM_ALPHA = (2.0 * DEPTH) ** 0.25
DEEPNORM_BETA = (8.0 * DEPTH) ** -0.25
IN_SIZES = (GLA_HEADS * GLA_DK, GLA_HEADS * GLA_DK, GLA_HEADS * GLA_DV, GLA_HEADS * GLA_DV,
            2 * GLA_LOWRANK, NA_HEADS * NA_DH, NA_HEADS * NA_DH, NA_HEADS * NA_DH)
IN_W = sum(IN_SIZES)

kernel_name = 'hymba_gla_natten_deepnorm_encoder'


def _split_points(sizes):
    pts, acc = [], 0
    for s in sizes[:-1]:
        acc += s
        pts.append(acc)
    return pts


def layer_norm(x, g, b):
    xf = x.astype(jnp.float32)
    mu = jnp.mean(xf, axis=-1, keepdims=True)
    var = jnp.mean(jnp.square(xf - mu), axis=-1, keepdims=True)
    y = (xf - mu) * lax.rsqrt(var + LN_EPS) * g.astype(jnp.float32) + b.astype(jnp.float32)
    return y.astype(x.dtype)


def to_heads(t, n_heads):
    B, T, _ = t.shape
    return t.reshape(B, T, n_heads, -1).transpose(0, 2, 1, 3)


def gla_chunked(q, k, v, log_a):
    B, H, T, dk = q.shape
    dv = v.shape[-1]
    n = T // GLA_CHUNK
    q = q.reshape(B, H, n, GLA_CHUNK, dk)
    k = k.reshape(B, H, n, GLA_CHUNK, dk)
    v = v.reshape(B, H, n, GLA_CHUNK, dv)
    b = jnp.cumsum(log_a.reshape(B, H, n, GLA_CHUNK, dk), axis=3)
    b_last = b[:, :, :, -1:, :]
    q_in = q * jnp.exp(b)
    k_in = k * jnp.exp(-b)
    k_end = k * jnp.exp(b_last - b)
    lower = jnp.tril(jnp.ones((GLA_CHUNK, GLA_CHUNK), dtype=bool))
    a = jnp.where(lower, jnp.einsum('bhnid,bhnjd->bhnij', q_in, k_in), 0.0)
    o_intra = jnp.einsum('bhnij,bhnjv->bhniv', a, v)
    u = jnp.einsum('bhnjd,bhnjv->nbhdv', k_end, v)
    decay = jnp.exp(b_last[:, :, :, 0, :]).transpose(2, 0, 1, 3)

    def step(s, inp):
        d_c, u_c = inp
        return d_c[..., None] * s + u_c, s

    _, s_prev = lax.scan(step, jnp.zeros((B, H, dk, dv), q.dtype), (decay, u))
    o_inter = jnp.einsum('bhnid,nbhdv->bhniv', q_in, s_prev)
    return (o_intra + o_inter).reshape(B, H, T, dv)


def gla_mixer(q, k, v, g, lr, w_gate_fwd, b_gate_fwd, w_gate_bwd, b_gate_bwd, norm_g):
    dtype = q.dtype
    lr_f, lr_b = jnp.split(lr, 2, axis=-1)
    la_f = jax.nn.log_sigmoid((lr_f @ w_gate_fwd + b_gate_fwd).astype(jnp.float32)) / GLA_TAU
    la_b = jax.nn.log_sigmoid((lr_b @ w_gate_bwd + b_gate_bwd).astype(jnp.float32)) / GLA_TAU
    qh = to_heads(q, GLA_HEADS).astype(jnp.float32) * (GLA_DK ** -0.5)
    kh = to_heads(k, GLA_HEADS).astype(jnp.float32)
    vh = to_heads(v, GLA_HEADS).astype(jnp.float32)
    la_f = to_heads(la_f, GLA_HEADS)
    la_b = to_heads(la_b, GLA_HEADS)
    flip = lambda t: jnp.flip(t, axis=2)
    o = gla_chunked(qh, kh, vh, la_f) + flip(gla_chunked(flip(qh), flip(kh), flip(vh), flip(la_b)))
    o = o * lax.rsqrt(jnp.mean(jnp.square(o), axis=-1, keepdims=True) + RMS_EPS) * norm_g.astype(jnp.float32)
    B, H, T, dv = o.shape
    o = o.transpose(0, 2, 1, 3).reshape(B, T, H * dv).astype(dtype)
    return o * jax.nn.silu(g)


def neighbourhood_attention(q, k, v, rpb):
    B, T, _ = q.shape
    rows = T // GRID_W
    kr = min(NA_WIN_R, rows)
    ncb = GRID_W // NA_QBLOCK_C
    r = jnp.arange(rows)
    row_start = jnp.clip(r - kr // 2, 0, rows - kr)
    key_rows = row_start[:, None] + jnp.arange(kr)[None, :]
    qcols = jnp.arange(GRID_W).reshape(ncb, NA_QBLOCK_C)
    q_col_start = jnp.clip(qcols - NA_WIN_C // 2, 0, GRID_W - NA_WIN_C)
    band_start = jnp.clip(jnp.arange(ncb) * NA_QBLOCK_C - NA_WIN_C // 2, 0, GRID_W - NA_BAND_C)
    band_cols = band_start[:, None] + jnp.arange(NA_BAND_C)[None, :]
    col_in = ((band_cols[:, None, :] >= q_col_start[..., None]) &
              (band_cols[:, None, :] < q_col_start[..., None] + NA_WIN_C))
    dr = key_rows - r[:, None]
    dc = jnp.clip(band_cols[:, None, :] - qcols[..., None], -(NA_WIN_C - 1), NA_WIN_C - 1)
    bias = rpb[:, (dr + NA_WIN_R - 1)[:, None, None, :, None],
               (dc + NA_WIN_C - 1)[None, :, :, None, :]]
    qh = to_heads(q, NA_HEADS).reshape(B, NA_HEADS, rows, ncb, NA_QBLOCK_C, NA_DH)
    kh = to_heads(k, NA_HEADS).reshape(B, NA_HEADS, rows, GRID_W, NA_DH)
    vh = to_heads(v, NA_HEADS).reshape(B, NA_HEADS, rows, GRID_W, NA_DH)
    ri = key_rows[:, None, :, None]
    ci = band_cols[None, :, None, :]
    kg = kh[:, :, ri, ci]
    vg = vh[:, :, ri, ci]
    s = jnp.einsum('bhrnqd,bhrnkcd->bhrnqkc', qh, kg).astype(jnp.float32) * (NA_DH ** -0.5)
    s = s + bias.astype(jnp.float32)[None]
    s = jnp.where(col_in[None, None, None, :, :, None, :], s, -jnp.inf)
    p = jax.nn.softmax(s, axis=(-2, -1))
    o = jnp.einsum('bhrnqkc,bhrnkcd->bhrnqd', p.astype(v.dtype), vg)
    return o.reshape(B, NA_HEADS, T, NA_DH).transpose(0, 2, 1, 3).reshape(B, T, NA_HEADS * NA_DH)


def conv_gated_mlp(x, w_up, conv_w, conv_b, w_down):
    h = x @ w_up
    zero = jnp.zeros_like(h[:, :1])
    h_prev = jnp.concatenate([zero, h[:, :-1]], axis=1)
    h_next = jnp.concatenate([h[:, 1:], zero], axis=1)
    h = h_prev * conv_w[0] + h * conv_w[1] + h_next * conv_w[2] + conv_b
    a, gate = jnp.split(h, 2, axis=-1)
    return (jax.nn.silu(gate) * a) @ w_down


def encoder_trunk(x, ln_in_g, ln_in_b, w_in, gla_gate_w_fwd, gla_gate_b_fwd, gla_gate_w_bwd,
                  gla_gate_b_bwd, gla_norm_g, na_rpb, w_out, ln1_g, ln1_b, w_up, conv_w, conv_b,
                  w_down, ln2_g, ln2_b):
    x = layer_norm(x, ln_in_g, ln_in_b)
    pts = _split_points(IN_SIZES)
    for l in range(DEPTH):
        proj = x @ w_in[l]
        gq, gk, gv, gg, glr, nq, nk, nv = jnp.split(proj, pts, axis=-1)
        o_gla = gla_mixer(gq, gk, gv, gg, glr, gla_gate_w_fwd[l], gla_gate_b_fwd[l],
                          gla_gate_w_bwd[l], gla_gate_b_bwd[l], gla_norm_g[l])
        o_na = neighbourhood_attention(nq, nk, nv, na_rpb[l])
        mix = jnp.concatenate([o_gla, o_na], axis=-1) @ w_out[l]
        x = layer_norm(DEEPNORM_ALPHA * x + mix, ln1_g[l], ln1_b[l])
        f = conv_gated_mlp(x, w_up[l], conv_w[l], conv_b[l], w_down[l])
        x = layer_norm(DEEPNORM_ALPHA * x + f, ln2_g[l], ln2_b[l])
    return x


def setup_inputs(seed: int = 0) -> dict:
    key = jax.random.key(seed)
    ks = jax.random.split(key, 24)
    nrm = lambda k, shape, scale: jax.random.normal(k, shape, jnp.float32) * scale
    gain = lambda k, shape: 1.0 + 0.02 * jax.random.normal(k, shape, jnp.float32)
    return {
        'x_prompt': nrm(ks[0], (BATCH, SEQ, D_MODEL), 1.0),
        'x_sample': nrm(ks[1], (DEC_BATCH, DEC_SEQ, D_MODEL), 1.0),
        'ln_in_g': gain(ks[2], (D_MODEL,)),
        'ln_in_b': nrm(ks[3], (D_MODEL,), 0.02),
        'w_in': nrm(ks[4], (DEPTH, D_MODEL, IN_W), D_MODEL ** -0.5),
        'gla_gate_w_fwd': nrm(ks[5], (DEPTH, GLA_LOWRANK, GLA_HEADS * GLA_DK), GLA_LOWRANK ** -0.5),
        'gla_gate_b_fwd': nrm(ks[6], (DEPTH, GLA_HEADS * GLA_DK), 0.1),
        'gla_gate_w_bwd': nrm(ks[7], (DEPTH, GLA_LOWRANK, GLA_HEADS * GLA_DK), GLA_LOWRANK ** -0.5),
        'gla_gate_b_bwd': nrm(ks[8], (DEPTH, GLA_HEADS * GLA_DK), 0.1),
        'gla_norm_g': gain(ks[9], (DEPTH, GLA_DV)),
        'na_rpb': nrm(ks[10], (DEPTH, NA_HEADS, 2 * NA_WIN_R - 1, 2 * NA_WIN_C - 1), 0.1),
        'w_out': nrm(ks[11], (DEPTH, MIX_W, D_MODEL), DEEPNORM_BETA * MIX_W ** -0.5),
        'ln1_g': gain(ks[12], (DEPTH, D_MODEL)),
        'ln1_b': nrm(ks[13], (DEPTH, D_MODEL), 0.02),
        'w_up': nrm(ks[14], (DEPTH, D_MODEL, 2 * D_FF), D_MODEL ** -0.5),
        'conv_w': nrm(ks[15], (DEPTH, CONV_W, 2 * D_FF), CONV_W ** -0.5),
        'conv_b': nrm(ks[16], (DEPTH, 2 * D_FF), 0.02),
        'w_down': nrm(ks[17], (DEPTH, D_FF, D_MODEL), DEEPNORM_BETA * D_FF ** -0.5),
        'ln2_g': gain(ks[18], (DEPTH, D_MODEL)),
        'ln2_b': nrm(ks[19], (DEPTH, D_MODEL), 0.02),
    }


def reference(x_prompt, x_sample, ln_in_g, ln_in_b, w_in, gla_gate_w_fwd, gla_gate_b_fwd,
              gla_gate_w_bwd, gla_gate_b_bwd, gla_norm_g, na_rpb, w_out, ln1_g, ln1_b, w_up,
              conv_w, conv_b, w_down, ln2_g, ln2_b):
    y_prompt = encoder_trunk(x_prompt, ln_in_g, ln_in_b, w_in, gla_gate_w_fwd, gla_gate_b_fwd,
                             gla_gate_w_bwd, gla_gate_b_bwd, gla_norm_g, na_rpb, w_out, ln1_g,
                             ln1_b, w_up, conv_w, conv_b, w_down, ln2_g, ln2_b)
    y_sample = encoder_trunk(x_sample, ln_in_g, ln_in_b, w_in, gla_gate_w_fwd, gla_gate_b_fwd,
                             gla_gate_w_bwd, gla_gate_b_bwd, gla_norm_g, na_rpb, w_out, ln1_g,
                             ln1_b, w_up, conv_w, conv_b, w_down, ln2_g, ln2_b)
    return (y_prompt, y_sample)
```

```python
import functools

import numpy as np
import jax
import jax.numpy as jnp
from jax import lax
from jax.experimental import pallas as pl
from jax.experimental.pallas import tpu as pltpu

GRID_W = 64
GLA_HEADS, GLA_DK, GLA_DV, GLA_LOWRANK, GLA_TAU, GLA_CHUNK = 4, 64, 128, 16, 16.0, 64
NA_HEADS, NA_DH, NA_WIN_R, NA_WIN_C = 8, 64, 8, 16
LN_EPS, RMS_EPS = 1e-5, 1e-6
DEPTH = 1
DEEPNORM_ALPHA = (2.0 * DEPTH) ** 0.25

LANES = 128
SUBLANES_F32 = 8
VMEM_LIMIT_BYTES = 48 * 1024 * 1024

TOK_TILE = 512
GLA_BLOCK = 512
SUPER = 2 * GLA_CHUNK
NA_QROWS = 2
NA_WBLOCKS = 5
FF_BLOCK = 256
HALO = 16

_NT = (((1,), (1,)), ((), ()))


def _layer_norm(x, g, b):
    mu = jnp.mean(x, axis=-1, keepdims=True)
    xc = x - mu
    var = jnp.mean(xc * xc, axis=-1, keepdims=True)
    return xc * lax.rsqrt(var + LN_EPS) * g + b


def _cparams(sem):
    return pltpu.CompilerParams(dimension_semantics=sem, vmem_limit_bytes=VMEM_LIMIT_BYTES)


_A_GQ, _A_GK, _A_GG, _A_LR, _A_NQ, _A_NV, _A_END = 0, 256, 512, 1024, 1152, 1664, 2176


def _in_proj_kernel(x_ref, g_ref, b_ref, wa_ref, wt_ref,
                    gq_ref, gk_ref, gg_ref, lr_ref, nq_ref, nv_ref, gvt_ref, nkt_ref):
    xn = _layer_norm(x_ref[0], g_ref[...], b_ref[...]).astype(jnp.bfloat16)

    def proj(lo, hi, out_ref):
        out_ref[0] = jnp.dot(xn, wa_ref[:, lo:hi], preferred_element_type=jnp.float32).astype(out_ref.dtype)

    proj(_A_GQ, _A_GK, gq_ref)
    proj(_A_GK, _A_GG, gk_ref)
    proj(_A_GG, _A_LR, gg_ref)
    proj(_A_LR, _A_NQ, lr_ref)
    proj(_A_NQ, _A_NV, nq_ref)
    proj(_A_NV, _A_END, nv_ref)
    gvt_ref[0] = lax.dot_general(wt_ref[0:512, :], xn, _NT,
                                 preferred_element_type=jnp.float32).astype(gvt_ref.dtype)
    nkt_ref[0] = lax.dot_general(wt_ref[512:1024, :], xn, _NT,
                                 preferred_element_type=jnp.float32).astype(nkt_ref.dtype)


def _in_proj(x, g, b, wa, wt):
    B, T, D = x.shape
    tm = TOK_TILE
    tok = lambda n: pl.BlockSpec((1, tm, n), lambda bi, ti: (bi, ti, 0))
    tr = pl.BlockSpec((1, 512, tm), lambda bi, ti: (bi, 0, ti))
    full = lambda a: pl.BlockSpec(a.shape, lambda bi, ti: (0,) * a.ndim)
    bf = jnp.bfloat16
    out_shape = (
        jax.ShapeDtypeStruct((B, T, 256), bf), jax.ShapeDtypeStruct((B, T, 256), bf),
        jax.ShapeDtypeStruct((B, T, 512), bf), jax.ShapeDtypeStruct((B, T, 128), bf),
        jax.ShapeDtypeStruct((B, T, 512), bf), jax.ShapeDtypeStruct((B, T, 512), bf),
        jax.ShapeDtypeStruct((B, 512, T), bf), jax.ShapeDtypeStruct((B, 512, T), bf),
    )
    return pl.pallas_call(
        _in_proj_kernel,
        grid=(B, T // tm),
        in_specs=[tok(D), full(g), full(b), full(wa), full(wt)],
        out_specs=(tok(256), tok(256), tok(512), tok(128), tok(512), tok(512), tr, tr),
        out_shape=out_shape,
        compiler_params=_cparams(("parallel", "parallel")),
        name="in_proj",
    )(x, g, b, wa, wt)


def _log_sigmoid(z):
    return jnp.minimum(z, 0.0) - jnp.log1p(jnp.exp(-jnp.abs(z)))


def _gla_kernel(*refs, reverse, final):
    if final:
        (q_ref, k_ref, vt_ref, lr_ref, wg_ref, bg_ref, gg_ref, ob_ref, ng_ref, out_ref, st_ref) = refs
    else:
        (q_ref, k_ref, vt_ref, lr_ref, wg_ref, bg_ref, out_ref, st_ref) = refs
    C = GLA_CHUNK
    nsc = q_ref.shape[1] // SUPER

    @pl.when(pl.program_id(1) == 0)
    def _():
        st_ref[...] = jnp.zeros_like(st_ref)

    ri = lax.broadcasted_iota(jnp.int32, (C, C), 0)
    ci = lax.broadcasted_iota(jnp.int32, (C, C), 1)
    cum = (ci >= ri) if reverse else (ci <= ri)
    tri = jnp.where(cum, 1.0, 0.0).astype(jnp.bfloat16)
    ri2 = lax.broadcasted_iota(jnp.int32, (C, SUPER), 0)
    ci2 = lax.broadcasted_iota(jnp.int32, (C, SUPER), 1)
    lane_pair = lax.broadcasted_iota(jnp.int32, (C, LANES), 1)
    head_lane = [lane_pair < GLA_DK, lane_pair >= GLA_DK]
    lane_state = lax.broadcasted_iota(jnp.int32, (LANES, LANES), 1) < GLA_DK
    zeros_c = jnp.zeros((C, LANES), jnp.bfloat16)
    amask = []
    for cc in range(2):
        cj = ci2 - cc * C
        inside = (cj >= 0) & (cj < C)
        amask.append(inside & ((cj >= ri2) if reverse else (cj <= ri2)))
    wg = wg_ref[...]
    bg = bg_ref[...]

    def super_chunk(s, carry):
        sc = (nsc - 1 - s) if reverse else s
        base = pl.multiple_of(sc * SUPER, SUPER)
        vt_sc = vt_ref[0, :, pl.ds(base, SUPER)]
        for step in range(2):
            cc = (1 - step) if reverse else step
            rows = pl.ds(base + cc * C, C)
            z = jnp.dot(lr_ref[0, rows, :], wg, preferred_element_type=jnp.float32) + bg
            la = _log_sigmoid(z) * (1.0 / GLA_TAU)
            la_hi = la.astype(jnp.bfloat16)
            la_lo = (la - la_hi.astype(jnp.float32)).astype(jnp.bfloat16)
            bcum = (jnp.dot(tri, la_hi, preferred_element_type=jnp.float32)
                    + jnp.dot(tri, la_lo, preferred_element_type=jnp.float32))
            btot = bcum[0:1, :] if reverse else bcum[C - 1:C, :]
            dec = jnp.exp(btot)
            q_in = (q_ref[0, rows, :].astype(jnp.float32) * jnp.exp(bcum)).astype(jnp.bfloat16)
            k_in32 = k_ref[0, rows, :].astype(jnp.float32) * jnp.exp(-bcum)
            k_in = k_in32.astype(jnp.bfloat16)
            k_end = (k_in32 * dec).astype(jnp.bfloat16)
            for g in range(2):
                lanes = slice(g * LANES, (g + 1) * LANES)
                qp, kp, kep = q_in[:, lanes], k_in[:, lanes], k_end[:, lanes]
                pad = (lambda a: jnp.concatenate([a, zeros_c], axis=0)) if cc == 0 else \
                      (lambda a: jnp.concatenate([zeros_c, a], axis=0))
                st_prev = st_ref[g]
                st_bf = st_prev.astype(jnp.bfloat16)
                kin_pad = pad(kp)
                for hh in range(2):
                    h = 2 * g + hh
                    qm = jnp.where(head_lane[hh], qp, jnp.zeros_like(qp))
                    a = lax.dot_general(qm, kin_pad, _NT, preferred_element_type=jnp.float32)
                    a = jnp.where(amask[cc], a, 0.0).astype(jnp.bfloat16)
                    lhs = jnp.concatenate([qm, a], axis=1)
                    rhs = jnp.concatenate([st_bf, vt_sc[h * GLA_DV:(h + 1) * GLA_DV, :]], axis=1)
                    o = lax.dot_general(lhs, rhs, _NT, preferred_element_type=jnp.float32)
                    cols = slice(h * GLA_DV, (h + 1) * GLA_DV)
                    if final:
                        o = o + ob_ref[0, rows, cols]
                        o = o * lax.rsqrt(jnp.mean(o * o, axis=-1, keepdims=True) + RMS_EPS) * ng_ref[...]
                        gate = gg_ref[0, rows, cols].astype(jnp.float32)
                        o = o * (gate * (1.0 / (1.0 + jnp.exp(-gate))))
                    out_ref[0, rows, cols] = o.astype(out_ref.dtype)
                ut_all = jnp.dot(vt_sc[g * 2 * GLA_DV:(g + 1) * 2 * GLA_DV, :], pad(kep),
                                 preferred_element_type=jnp.float32)
                ut = jnp.where(lane_state, ut_all[0:GLA_DV], ut_all[GLA_DV:2 * GLA_DV])
                st_ref[g] = st_prev * dec[:, lanes] + ut
        return carry

    lax.fori_loop(0, nsc, super_chunk, 0)


def _gla(q, k, vt, lr, wg, bg, *, reverse, gg=None, ob=None, ng=None):
    B, T, _ = q.shape
    tb = GLA_BLOCK
    nb = T // tb
    final = gg is not None
    blk = (lambda bi, i: (bi, nb - 1 - i, 0)) if reverse else (lambda bi, i: (bi, i, 0))
    blk_t = (lambda bi, i: (bi, 0, nb - 1 - i)) if reverse else (lambda bi, i: (bi, 0, i))
    tok = lambda n: pl.BlockSpec((1, tb, n), blk)
    full = lambda a: pl.BlockSpec(a.shape, lambda bi, i: (0,) * a.ndim)
    in_specs = [tok(256), tok(256), pl.BlockSpec((1, 512, tb), blk_t), tok(128), full(wg), full(bg)]
    args = [q, k, vt, lr, wg, bg]
    if final:
        in_specs += [tok(512), tok(512), full(ng)]
        args += [gg, ob, ng]
    return pl.pallas_call(
        functools.partial(_gla_kernel, reverse=reverse, final=final),
        grid=(B, nb),
        in_specs=in_specs,
        out_specs=tok(512),
        out_shape=jax.ShapeDtypeStruct((B, T, 512), jnp.bfloat16 if final else jnp.float32),
        scratch_shapes=[pltpu.VMEM((2, GLA_DV, 2 * GLA_DK), jnp.float32)],
        compiler_params=_cparams(("parallel", "arbitrary")),
        name="gla_fwd" if final else "gla_bwd",
    )(*args)


def _na_window_block(p, npairs):
    return jnp.clip(p - 2, 0, npairs - NA_WBLOCKS)


def _na_pair_type(p, npairs):
    return jnp.where(p < 2, p, jnp.where(p >= npairs - 2, p - (npairs - 5), 2))


def _na_bias_indices():
    rows = 64
    npairs = rows // 2
    wk = NA_WBLOCKS * NA_QROWS
    dr_idx = np.zeros((5, NA_QROWS * GRID_W, wk * GRID_W), np.int32)
    dc_idx = np.zeros_like(dr_idx)
    valid = np.zeros(dr_idx.shape, bool)
    c = np.arange(GRID_W)
    qcs = np.clip(c - NA_WIN_C // 2, 0, GRID_W - NA_WIN_C)
    for ty, p in enumerate((0, 1, 2, npairs - 2, npairs - 1)):
        w0 = 2 * int(np.clip(p - 2, 0, npairs - NA_WBLOCKS))
        for a in range(NA_QROWS):
            r = 2 * p + a
            rs = int(np.clip(r - NA_WIN_R // 2, 0, rows - NA_WIN_R))
            for i in range(wk):
                kr = w0 + i
                dr = kr - r
                row_ok = rs <= kr < rs + NA_WIN_R
                dc = c[None, :] - c[:, None]
                col_ok = (c[None, :] >= qcs[:, None]) & (c[None, :] < qcs[:, None] + NA_WIN_C)
                qs = slice(a * GRID_W, (a + 1) * GRID_W)
                ks = slice(i * GRID_W, (i + 1) * GRID_W)
                valid[ty, qs, ks] = col_ok & row_ok
                dr_idx[ty, qs, ks] = np.clip(dr + NA_WIN_R - 1, 0, 2 * NA_WIN_R - 2)
                dc_idx[ty, qs, ks] = np.clip(dc + NA_WIN_C - 1, 0, 2 * NA_WIN_C - 2)
    return dr_idx, dc_idx, valid


def _natten_kernel(*refs):
    q_ref = refs[0]
    kt_refs = refs[1:1 + NA_WBLOCKS]
    v_refs = refs[1 + NA_WBLOCKS:1 + 2 * NA_WBLOCKS]
    bias_ref = refs[1 + 2 * NA_WBLOCKS]
    out_ref = refs[2 + 2 * NA_WBLOCKS]
    nq = q_ref.shape[1]
    lane = lax.broadcasted_iota(jnp.int32, (nq, LANES), 1)
    head_lane = [lane < NA_DH, lane >= NA_DH]
    for g in range(NA_HEADS // 2):
        lanes = slice(g * LANES, (g + 1) * LANES)
        qp = q_ref[0, :, lanes]
        kt = jnp.concatenate([r[0, lanes, :] for r in kt_refs], axis=1)
        v = jnp.concatenate([r[0, :, lanes] for r in v_refs], axis=0)
        outs = []
        for hh in range(2):
            qm = jnp.where(head_lane[hh], qp, jnp.zeros_like(qp))
            s = jnp.dot(qm, kt, preferred_element_type=jnp.float32) + bias_ref[0, 2 * g + hh]
            m = jnp.max(s, axis=-1, keepdims=True)
            e = jnp.exp(s - m)
            l = jnp.sum(e, axis=-1, keepdims=True)
            pv = jnp.dot(e.astype(jnp.bfloat16), v, preferred_element_type=jnp.float32)
            outs.append(pv * (1.0 / l))
        out_ref[0, :, lanes] = jnp.where(head_lane[0], outs[0], outs[1]).astype(out_ref.dtype)


def _natten(q, kt, v, bias):
    B, T, _ = q.shape
    nq = NA_QROWS * GRID_W
    npairs = T // nq
    q_spec = pl.BlockSpec((1, nq, 512), lambda bi, p: (bi, p, 0))
    kt_specs = [pl.BlockSpec((1, 512, nq), functools.partial(
        lambda bi, p, j: (bi, 0, _na_window_block(p, npairs) + j), j=j)) for j in range(NA_WBLOCKS)]
    v_specs = [pl.BlockSpec((1, nq, 512), functools.partial(
        lambda bi, p, j: (bi, _na_window_block(p, npairs) + j, 0), j=j)) for j in range(NA_WBLOCKS)]
    bias_spec = pl.BlockSpec((1,) + bias.shape[1:], lambda bi, p: (_na_pair_type(p, npairs), 0, 0, 0))
    return pl.pallas_call(
        _natten_kernel,
        grid=(B, npairs),
        in_specs=[q_spec] + kt_specs + v_specs + [bias_spec],
        out_specs=q_spec,
        out_shape=jax.ShapeDtypeStruct((B, T, 512), jnp.bfloat16),
        compiler_params=_cparams(("parallel", "arbitrary")),
        name="natten",
    )(q, *([kt] * NA_WBLOCKS), *([v] * NA_WBLOCKS), bias)


def _out_proj_kernel(x_ref, og_ref, on_ref, w_ref, gi_ref, bi_ref, g1_ref, b1_ref, out_ref):
    xn = _layer_norm(x_ref[0], gi_ref[...], bi_ref[...])
    mix = (jnp.dot(og_ref[0], w_ref[0:512, :], preferred_element_type=jnp.float32)
           + jnp.dot(on_ref[0], w_ref[512:1024, :], preferred_element_type=jnp.float32))
    out_ref[0] = _layer_norm(DEEPNORM_ALPHA * xn + mix, g1_ref[...], b1_ref[...])


def _out_proj(x, og, on, w, gi, bi, g1, b1):
    B, T, D = x.shape
    tm = TOK_TILE
    tok = lambda n: pl.BlockSpec((1, tm, n), lambda b_, t: (b_, t, 0))
    full = lambda a: pl.BlockSpec(a.shape, lambda b_, t: (0,) * a.ndim)
    return pl.pallas_call(
        _out_proj_kernel,
        grid=(B, T // tm),
        in_specs=[tok(D), tok(512), tok(512), full(w), full(gi), full(bi), full(g1), full(b1)],
        out_specs=tok(D),
        out_shape=jax.ShapeDtypeStruct((B, T, D), jnp.float32),
        compiler_params=_cparams(("parallel", "parallel")),
        name="out_proj",
    )(x, og, on, w, gi, bi, g1, b1)


def _mlp_kernel(x_ref, top_ref, bot_ref, wa_ref, wg_ref, cwa_ref, cwg_ref, cba_ref, cbg_ref, wd_ref,
                g2_ref, b2_ref, out_ref, xb_ref, acc_ref, ha_ref, hg_ref):
    t = pl.program_id(1)
    f = pl.program_id(2)
    tm = x_ref.shape[1]

    @pl.when(f == 0)
    def _():
        acc_ref[...] = jnp.zeros_like(acc_ref)
        zeros = jnp.zeros((HALO - SUBLANES_F32, x_ref.shape[2]), jnp.float32)
        top = jnp.where(t > 0, top_ref[0], 0.0)
        bot = jnp.where(t < pl.num_programs(1) - 1, bot_ref[0], 0.0)
        xb_ref[0:HALO, :] = jnp.concatenate([zeros, top], axis=0).astype(jnp.bfloat16)
        xb_ref[HALO:HALO + tm, :] = x_ref[0].astype(jnp.bfloat16)
        xb_ref[HALO + tm:, :] = jnp.concatenate([bot, zeros], axis=0).astype(jnp.bfloat16)

    xb = xb_ref[...]
    ha_ref[...] = jnp.dot(xb, wa_ref[...], preferred_element_type=jnp.float32)
    hg_ref[...] = jnp.dot(xb, wg_ref[...], preferred_element_type=jnp.float32)

    def conv(h_ref, cw_ref, cb_ref):
        return (h_ref[HALO - 1:HALO - 1 + tm, :] * cw_ref[0:1, :] + h_ref[HALO:HALO + tm, :] * cw_ref[1:2, :]
                + h_ref[HALO + 1:HALO + 1 + tm, :] * cw_ref[2:3, :] + cb_ref[...])

    a = conv(ha_ref, cwa_ref, cba_ref)
    gt = conv(hg_ref, cwg_ref, cbg_ref)
    act = (gt * (1.0 / (1.0 + jnp.exp(-gt))) * a).astype(jnp.bfloat16)
    acc_ref[...] += jnp.dot(act, wd_ref[...], preferred_element_type=jnp.float32)

    @pl.when(f == pl.num_programs(2) - 1)
    def _():
        out_ref[0] = _layer_norm(DEEPNORM_ALPHA * x_ref[0] + acc_ref[...], g2_ref[...], b2_ref[...])


def _mlp(x, w_up, conv_w, conv_b, w_down, g2, b2):
    B, T, D = x.shape
    tm = TOK_TILE
    d_ff = w_down.shape[0]
    nf = d_ff // FF_BLOCK
    hb = tm // SUBLANES_F32
    last_hb = T // SUBLANES_F32 - 1
    tok = pl.BlockSpec((1, tm, D), lambda b_, t, f: (b_, t, 0))
    top = pl.BlockSpec((1, SUBLANES_F32, D), lambda b_, t, f: (b_, jnp.maximum(t * hb - 1, 0), 0))
    bot = pl.BlockSpec((1, SUBLANES_F32, D), lambda b_, t, f: (b_, jnp.minimum((t + 1) * hb, last_hb), 0))
    col_a = lambda r: pl.BlockSpec((r, FF_BLOCK), lambda b_, t, f: (0, f))
    col_g = lambda r: pl.BlockSpec((r, FF_BLOCK), lambda b_, t, f: (0, nf + f))
    full = lambda a: pl.BlockSpec(a.shape, lambda b_, t, f: (0,) * a.ndim)
    return pl.pallas_call(
        _mlp_kernel,
        grid=(B, T // tm, nf),
        in_specs=[tok, top, bot, col_a(D), col_g(D), col_a(3), col_g(3), col_a(1), col_g(1),
                  pl.BlockSpec((FF_BLOCK, D), lambda b_, t, f: (f, 0)), full(g2), full(b2)],
        out_specs=tok,
        out_shape=jax.ShapeDtypeStruct((B, T, D), jnp.float32),
        scratch_shapes=[pltpu.VMEM((tm + 2 * HALO, D), jnp.bfloat16), pltpu.VMEM((tm, D), jnp.float32),
                        pltpu.VMEM((tm + 2 * HALO, FF_BLOCK), jnp.float32),
                        pltpu.VMEM((tm + 2 * HALO, FF_BLOCK), jnp.float32)],
        compiler_params=_cparams(("parallel", "parallel", "arbitrary")),
        name="mlp",
    )(x, x, x, w_up, w_up, conv_w, conv_w, conv_b, conv_b, w_down, g2, b2)


def _prepare_weights(ln_in_g, ln_in_b, w_in, gw_f, gb_f, gw_b, gb_b, gla_norm_g, na_rpb, w_out,
                     ln1_g, ln1_b, w_up, conv_w, conv_b, w_down, ln2_g, ln2_b):
    bf = jnp.bfloat16
    row = lambda a: a.reshape(1, -1).astype(jnp.float32)
    w = w_in[0]
    sizes = (256, 256, 512, 512, 32, 512, 512, 512)
    offs = np.concatenate([[0], np.cumsum(sizes)])
    gq, gk, gv, gg, glr, nq, nk, nv = [w[:, offs[i]:offs[i + 1]] for i in range(8)]
    lr_pad = jnp.zeros((w.shape[0], LANES - 2 * GLA_LOWRANK), w.dtype)
    wa = jnp.concatenate([gq * (GLA_DK ** -0.5), gk, gg, glr, lr_pad, nq * (NA_DH ** -0.5), nv], axis=1).astype(bf)
    wt = jnp.concatenate([gv, nk], axis=1).T.astype(bf)
    zpad = jnp.zeros((LANES - 2 * GLA_LOWRANK, gw_f.shape[-1]), jnp.float32)
    zlr = jnp.zeros((GLA_LOWRANK, gw_f.shape[-1]), jnp.float32)
    wg_f = jnp.concatenate([gw_f[0], zlr, zpad], axis=0).astype(bf)
    wg_b = jnp.concatenate([zlr, gw_b[0], zpad], axis=0).astype(bf)
    dr_idx, dc_idx, valid = _na_bias_indices()
    bias = jnp.where(valid[None], na_rpb[0][:, dr_idx, dc_idx], -jnp.inf)
    bias = jnp.transpose(bias, (1, 0, 2, 3)).astype(jnp.float32)
    return dict(
        ln_in=(row(ln_in_g), row(ln_in_b)), wa=wa, wt=wt,
        wg_f=wg_f, bg_f=row(gb_f[0]), wg_b=wg_b, bg_b=row(gb_b[0]), ng=row(gla_norm_g[0]),
        bias=bias, w_out=w_out[0].astype(bf), ln1=(row(ln1_g[0]), row(ln1_b[0])),
        w_up=w_up[0].astype(bf), conv_w=conv_w[0].astype(jnp.float32), conv_b=row(conv_b[0]),
        w_down=w_down[0].astype(bf), ln2=(row(ln2_g[0]), row(ln2_b[0])),
    )


def _trunk(x, p):
    gq, gk, gg, lr, nq, nv, gvt, nkt = _in_proj(x, *p["ln_in"], p["wa"], p["wt"])
    o_bwd = _gla(gq, gk, gvt, lr, p["wg_b"], p["bg_b"], reverse=True)
    o_gla = _gla(gq, gk, gvt, lr, p["wg_f"], p["bg_f"], reverse=False, gg=gg, ob=o_bwd, ng=p["ng"])
    o_na = _natten(nq, nkt, nv, p["bias"])
    x1 = _out_proj(x, o_gla, o_na, p["w_out"], *p["ln_in"], *p["ln1"])
    return _mlp(x1, p["w_up"], p["conv_w"], p["conv_b"], p["w_down"], *p["ln2"])


def kernel(x_prompt, x_sample, ln_in_g, ln_in_b, w_in, gla_gate_w_fwd, gla_gate_b_fwd, gla_gate_w_bwd,
           gla_gate_b_bwd, gla_norm_g, na_rpb, w_out, ln1_g, ln1_b, w_up, conv_w, conv_b, w_down, ln2_g, ln2_b):
    p = _prepare_weights(ln_in_g, ln_in_b, w_in, gla_gate_w_fwd, gla_gate_b_fwd, gla_gate_w_bwd,
                         gla_gate_b_bwd, gla_norm_g, na_rpb, w_out, ln1_g, ln1_b, w_up, conv_w, conv_b,
                         w_down, ln2_g, ln2_b)
    return (_trunk(x_prompt, p), _trunk(x_sample, p))
```

```python
import functools

import numpy as np
import jax
import jax.numpy as jnp
from jax import lax
from jax.experimental import pallas as pl
from jax.experimental.pallas import tpu as pltpu

GRID_W = 64
GLA_HEADS, GLA_DK, GLA_DV, GLA_LOWRANK, GLA_TAU, GLA_CHUNK = 4, 64, 128, 16, 16.0, 64
NA_HEADS, NA_DH, NA_WIN_R, NA_WIN_C = 8, 64, 8, 16
LN_EPS, RMS_EPS = 1e-5, 1e-6
DEPTH = 1
DEEPNORM_ALPHA = (2.0 * DEPTH) ** 0.25

LANES = 128
SUBLANES_F32 = 8
VMEM_LIMIT_BYTES = 48 * 1024 * 1024
MLP_VMEM_LIMIT_BYTES = 56 * 1024 * 1024

TOK_TILE = 512
GLA_BLOCK = 512
SUPER = 2 * GLA_CHUNK
NA_QROWS = 2
NA_WBLOCKS = 5
FF_BLOCK = 256
HALO = 16

_NT = (((1,), (1,)), ((), ()))


def _layer_norm(x, g, b):
    mu = jnp.mean(x, axis=-1, keepdims=True)
    xc = x - mu
    var = jnp.mean(xc * xc, axis=-1, keepdims=True)
    return xc * lax.rsqrt(var + LN_EPS) * g + b


def _cparams(sem):
    return pltpu.CompilerParams(dimension_semantics=sem, vmem_limit_bytes=VMEM_LIMIT_BYTES)


_A_GQ, _A_GK, _A_GG, _A_LR, _A_NQ, _A_NV, _A_END = 0, 256, 512, 1024, 1152, 1664, 2176


def _in_proj_kernel(x_ref, g_ref, b_ref, wa_ref, wt_ref,
                    gq_ref, gk_ref, gg_ref, lr_ref, nq_ref, nv_ref, gvt_ref, nkt_ref):
    xn = _layer_norm(x_ref[0], g_ref[...], b_ref[...]).astype(jnp.bfloat16)

    def proj(lo, hi, out_ref):
        out_ref[0] = jnp.dot(xn, wa_ref[:, lo:hi], preferred_element_type=jnp.float32).astype(out_ref.dtype)

    proj(_A_GQ, _A_GK, gq_ref)
    proj(_A_GK, _A_GG, gk_ref)
    proj(_A_GG, _A_LR, gg_ref)
    proj(_A_LR, _A_NQ, lr_ref)
    proj(_A_NQ, _A_NV, nq_ref)
    proj(_A_NV, _A_END, nv_ref)
    gvt_ref[0] = lax.dot_general(wt_ref[0:512, :], xn, _NT,
                                 preferred_element_type=jnp.float32).astype(gvt_ref.dtype)
    nkt_ref[0] = lax.dot_general(wt_ref[512:1024, :], xn, _NT,
                                 preferred_element_type=jnp.float32).astype(nkt_ref.dtype)


def _in_proj(x, g, b, wa, wt):
    B, T, D = x.shape
    tm = TOK_TILE
    tok = lambda n: pl.BlockSpec((1, tm, n), lambda bi, ti: (bi, ti, 0))
    tr = pl.BlockSpec((1, 512, tm), lambda bi, ti: (bi, 0, ti))
    full = lambda a: pl.BlockSpec(a.shape, lambda bi, ti: (0,) * a.ndim)
    bf = jnp.bfloat16
    out_shape = (
        jax.ShapeDtypeStruct((B, T, 256), bf), jax.ShapeDtypeStruct((B, T, 256), bf),
        jax.ShapeDtypeStruct((B, T, 512), bf), jax.ShapeDtypeStruct((B, T, 128), bf),
        jax.ShapeDtypeStruct((B, T, 512), bf), jax.ShapeDtypeStruct((B, T, 512), bf),
        jax.ShapeDtypeStruct((B, 512, T), bf), jax.ShapeDtypeStruct((B, 512, T), bf),
    )
    return pl.pallas_call(
        _in_proj_kernel,
        grid=(B, T // tm),
        in_specs=[tok(D), full(g), full(b), full(wa), full(wt)],
        out_specs=(tok(256), tok(256), tok(512), tok(128), tok(512), tok(512), tr, tr),
        out_shape=out_shape,
        compiler_params=_cparams(("parallel", "parallel")),
        name="in_proj",
    )(x, g, b, wa, wt)


def _log_sigmoid(z):
    return jnp.minimum(z, 0.0) - jnp.log1p(jnp.exp(-jnp.abs(z)))


def _gla_kernel(*refs, reverse, final):
    if final:
        (q_ref, k_ref, vt_ref, lr_ref, wg_ref, bg_ref, gg_ref, ob_ref, ng_ref, out_ref, st_ref) = refs
    else:
        (q_ref, k_ref, vt_ref, lr_ref, wg_ref, bg_ref, out_ref, st_ref) = refs
    C = GLA_CHUNK
    nsc = q_ref.shape[1] // SUPER

    @pl.when(pl.program_id(1) == 0)
    def _():
        st_ref[...] = jnp.zeros_like(st_ref)

    ri = lax.broadcasted_iota(jnp.int32, (C, C), 0)
    ci = lax.broadcasted_iota(jnp.int32, (C, C), 1)
    cum = (ci >= ri) if reverse else (ci <= ri)
    tri = jnp.where(cum, 1.0, 0.0).astype(jnp.bfloat16)
    ri2 = lax.broadcasted_iota(jnp.int32, (C, SUPER), 0)
    ci2 = lax.broadcasted_iota(jnp.int32, (C, SUPER), 1)
    lane_pair = lax.broadcasted_iota(jnp.int32, (C, LANES), 1)
    head_lane = [lane_pair < GLA_DK, lane_pair >= GLA_DK]
    lane_state = lax.broadcasted_iota(jnp.int32, (LANES, LANES), 1) < GLA_DK
    zeros_c = jnp.zeros((C, LANES), jnp.bfloat16)
    amask = []
    for cc in range(2):
        cj = ci2 - cc * C
        inside = (cj >= 0) & (cj < C)
        amask.append(inside & ((cj >= ri2) if reverse else (cj <= ri2)))
    wg = wg_ref[...]
    bg = bg_ref[...]

    def super_chunk(s, carry):
        sc = (nsc - 1 - s) if reverse else s
        base = pl.multiple_of(sc * SUPER, SUPER)
        vt_sc = vt_ref[0, :, pl.ds(base, SUPER)]
        for step in range(2):
            cc = (1 - step) if reverse else step
            rows = pl.ds(base + cc * C, C)
            z = jnp.dot(lr_ref[0, rows, :], wg, preferred_element_type=jnp.float32) + bg
            la = _log_sigmoid(z) * (1.0 / GLA_TAU)
            la_hi = la.astype(jnp.bfloat16)
            la_lo = (la - la_hi.astype(jnp.float32)).astype(jnp.bfloat16)
            bcum = (jnp.dot(tri, la_hi, preferred_element_type=jnp.float32)
                    + jnp.dot(tri, la_lo, preferred_element_type=jnp.float32))
            btot = bcum[0:1, :] if reverse else bcum[C - 1:C, :]
            dec = jnp.exp(btot)
            q_in = (q_ref[0, rows, :].astype(jnp.float32) * jnp.exp(bcum)).astype(jnp.bfloat16)
            k_in32 = k_ref[0, rows, :].astype(jnp.float32) * jnp.exp(-bcum)
            k_in = k_in32.astype(jnp.bfloat16)
            k_end = (k_in32 * dec).astype(jnp.bfloat16)
            for g in range(2):
                lanes = slice(g * LANES, (g + 1) * LANES)
                qp, kp, kep = q_in[:, lanes], k_in[:, lanes], k_end[:, lanes]
                pad = (lambda a: jnp.concatenate([a, zeros_c], axis=0)) if cc == 0 else \
                      (lambda a: jnp.concatenate([zeros_c, a], axis=0))
                st_prev = st_ref[g]
                st_bf = st_prev.astype(jnp.bfloat16)
                kin_pad = pad(kp)
                for hh in range(2):
                    h = 2 * g + hh
                    qm = jnp.where(head_lane[hh], qp, jnp.zeros_like(qp))
                    a = lax.dot_general(qm, kin_pad, _NT, preferred_element_type=jnp.float32)
                    a = jnp.where(amask[cc], a, 0.0).astype(jnp.bfloat16)
                    lhs = jnp.concatenate([qm, a], axis=1)
                    rhs = jnp.concatenate([st_bf, vt_sc[h * GLA_DV:(h + 1) * GLA_DV, :]], axis=1)
                    o = lax.dot_general(lhs, rhs, _NT, preferred_element_type=jnp.float32)
                    cols = slice(h * GLA_DV, (h + 1) * GLA_DV)
                    if final:
                        o = o + ob_ref[0, rows, cols]
                        o = o * lax.rsqrt(jnp.mean(o * o, axis=-1, keepdims=True) + RMS_EPS) * ng_ref[...]
                        gate = gg_ref[0, rows, cols].astype(jnp.float32)
                        o = o * (gate * (1.0 / (1.0 + jnp.exp(-gate))))
                    out_ref[0, rows, cols] = o.astype(out_ref.dtype)
                ut_all = jnp.dot(vt_sc[g * 2 * GLA_DV:(g + 1) * 2 * GLA_DV, :], pad(kep),
                                 preferred_element_type=jnp.float32)
                ut = jnp.where(lane_state, ut_all[0:GLA_DV], ut_all[GLA_DV:2 * GLA_DV])
                st_ref[g] = st_prev * dec[:, lanes] + ut
        return carry

    lax.fori_loop(0, nsc, super_chunk, 0)


def _gla(q, k, vt, lr, wg, bg, *, reverse, gg=None, ob=None, ng=None):
    B, T, _ = q.shape
    tb = GLA_BLOCK
    nb = T // tb
    final = gg is not None
    blk = (lambda bi, i: (bi, nb - 1 - i, 0)) if reverse else (lambda bi, i: (bi, i, 0))
    blk_t = (lambda bi, i: (bi, 0, nb - 1 - i)) if reverse else (lambda bi, i: (bi, 0, i))
    tok = lambda n: pl.BlockSpec((1, tb, n), blk)
    full = lambda a: pl.BlockSpec(a.shape, lambda bi, i: (0,) * a.ndim)
    in_specs = [tok(256), tok(256), pl.BlockSpec((1, 512, tb), blk_t), tok(128), full(wg), full(bg)]
    args = [q, k, vt, lr, wg, bg]
    if final:
        in_specs += [tok(512), tok(512), full(ng)]
        args += [gg, ob, ng]
    return pl.pallas_call(
        functools.partial(_gla_kernel, reverse=reverse, final=final),
        grid=(B, nb),
        in_specs=in_specs,
        out_specs=tok(512),
        out_shape=jax.ShapeDtypeStruct((B, T, 512), jnp.bfloat16 if final else jnp.float32),
        scratch_shapes=[pltpu.VMEM((2, GLA_DV, 2 * GLA_DK), jnp.float32)],
        compiler_params=_cparams(("parallel", "arbitrary")),
        name="gla_fwd" if final else "gla_bwd",
    )(*args)


def _na_window_block(p, npairs):
    return jnp.clip(p - 2, 0, npairs - NA_WBLOCKS)


def _na_pair_type(p, npairs):
    return jnp.where(p < 2, p, jnp.where(p >= npairs - 2, p - (npairs - 5), 2))


def _na_bias_table(rpb):
    rows = 64
    npairs = rows // 2
    wk = NA_WBLOCKS * NA_QROWS
    nh = rpb.shape[0]
    ring = jnp.concatenate([rpb[..., NA_WIN_C - 1:], jnp.zeros(rpb.shape[:2] + (LANES - (2 * NA_WIN_C - 1),), rpb.dtype),
                            rpb[..., :NA_WIN_C - 1]], axis=-1)
    toep = jnp.tile(ring, (1, 1, GRID_W))[..., :GRID_W * (LANES - 1)]
    toep = toep.reshape(nh, rpb.shape[1], GRID_W, LANES - 1)[..., :GRID_W]
    c = np.arange(GRID_W)
    qcs = np.clip(c - NA_WIN_C // 2, 0, GRID_W - NA_WIN_C)
    col_ok = (c[None, :] >= qcs[:, None]) & (c[None, :] < qcs[:, None] + NA_WIN_C)
    toep = jnp.where(col_ok[None, None], toep, -jnp.inf)
    masked = jnp.full((nh, GRID_W, GRID_W), -jnp.inf, rpb.dtype)
    types = []
    for p in (0, 1, 2, npairs - 2, npairs - 1):
        w0 = 2 * int(np.clip(p - 2, 0, npairs - NA_WBLOCKS))
        qrows = []
        for a in range(NA_QROWS):
            r = 2 * p + a
            rs = int(np.clip(r - NA_WIN_R // 2, 0, rows - NA_WIN_R))
            blocks = [toep[:, w0 + i - r + NA_WIN_R - 1] if rs <= w0 + i < rs + NA_WIN_R else masked
                      for i in range(wk)]
            qrows.append(jnp.concatenate(blocks, axis=-1))
        types.append(jnp.concatenate(qrows, axis=-2))
    return jnp.stack(types, axis=0).astype(jnp.float32)


def _natten_kernel(*refs):
    q_ref = refs[0]
    kt_refs = refs[1:1 + NA_WBLOCKS]
    v_refs = refs[1 + NA_WBLOCKS:1 + 2 * NA_WBLOCKS]
    bias_ref = refs[1 + 2 * NA_WBLOCKS]
    out_ref = refs[2 + 2 * NA_WBLOCKS]
    nq = q_ref.shape[1]
    lane = lax.broadcasted_iota(jnp.int32, (nq, LANES), 1)
    head_lane = [lane < NA_DH, lane >= NA_DH]
    for g in range(NA_HEADS // 2):
        lanes = slice(g * LANES, (g + 1) * LANES)
        qp = q_ref[0, :, lanes]
        kt = jnp.concatenate([r[0, lanes, :] for r in kt_refs], axis=1)
        v = jnp.concatenate([r[0, :, lanes] for r in v_refs], axis=0)
        outs = []
        for hh in range(2):
            qm = jnp.where(head_lane[hh], qp, jnp.zeros_like(qp))
            s = jnp.dot(qm, kt, preferred_element_type=jnp.float32) + bias_ref[0, 2 * g + hh]
            m = jnp.max(s, axis=-1, keepdims=True)
            e = jnp.exp(s - m)
            l = jnp.sum(e, axis=-1, keepdims=True)
            pv = jnp.dot(e.astype(jnp.bfloat16), v, preferred_element_type=jnp.float32)
            outs.append(pv * (1.0 / l))
        out_ref[0, :, lanes] = jnp.where(head_lane[0], outs[0], outs[1]).astype(out_ref.dtype)


def _natten(q, kt, v, bias):
    B, T, _ = q.shape
    nq = NA_QROWS * GRID_W
    npairs = T // nq
    q_spec = pl.BlockSpec((1, nq, 512), lambda bi, p: (bi, p, 0))
    kt_specs = [pl.BlockSpec((1, 512, nq), functools.partial(
        lambda bi, p, j: (bi, 0, _na_window_block(p, npairs) + j), j=j)) for j in range(NA_WBLOCKS)]
    v_specs = [pl.BlockSpec((1, nq, 512), functools.partial(
        lambda bi, p, j: (bi, _na_window_block(p, npairs) + j, 0), j=j)) for j in range(NA_WBLOCKS)]
    bias_spec = pl.BlockSpec((1,) + bias.shape[1:], lambda bi, p: (_na_pair_type(p, npairs), 0, 0, 0))
    return pl.pallas_call(
        _natten_kernel,
        grid=(B, npairs),
        in_specs=[q_spec] + kt_specs + v_specs + [bias_spec],
        out_specs=q_spec,
        out_shape=jax.ShapeDtypeStruct((B, T, 512), jnp.bfloat16),
        compiler_params=_cparams(("parallel", "arbitrary")),
        name="natten",
    )(q, *([kt] * NA_WBLOCKS), *([v] * NA_WBLOCKS), bias)


def _out_proj_kernel(x_ref, og_ref, on_ref, w_ref, gi_ref, bi_ref, g1_ref, b1_ref, out_ref):
    xn = _layer_norm(x_ref[0], gi_ref[...], bi_ref[...])
    mix = (jnp.dot(og_ref[0], w_ref[0:512, :], preferred_element_type=jnp.float32)
           + jnp.dot(on_ref[0], w_ref[512:1024, :], preferred_element_type=jnp.float32))
    out_ref[0] = _layer_norm(DEEPNORM_ALPHA * xn + mix, g1_ref[...], b1_ref[...])


def _out_proj(x, og, on, w, gi, bi, g1, b1):
    B, T, D = x.shape
    tm = TOK_TILE
    tok = lambda n: pl.BlockSpec((1, tm, n), lambda b_, t: (b_, t, 0))
    full = lambda a: pl.BlockSpec(a.shape, lambda b_, t: (0,) * a.ndim)
    return pl.pallas_call(
        _out_proj_kernel,
        grid=(B, T // tm),
        in_specs=[tok(D), tok(512), tok(512), full(w), full(gi), full(bi), full(g1), full(b1)],
        out_specs=tok(D),
        out_shape=jax.ShapeDtypeStruct((B, T, D), jnp.float32),
        compiler_params=_cparams(("parallel", "parallel")),
        name="out_proj",
    )(x, og, on, w, gi, bi, g1, b1)


def _mlp_kernel(x_ref, top_ref, bot_ref, wup_ref, cw_ref, cb_ref, wd_ref, g2_ref, b2_ref, out_ref,
                xb_ref, act_ref, ha_ref, hg_ref):
    t = pl.program_id(1)
    tm = x_ref.shape[1]
    d_ff = wd_ref.shape[0]

    zeros = jnp.zeros((HALO - SUBLANES_F32, x_ref.shape[2]), jnp.float32)
    top = jnp.where(t > 0, top_ref[0], 0.0)
    bot = jnp.where(t < pl.num_programs(1) - 1, bot_ref[0], 0.0)
    xb_ref[0:HALO, :] = jnp.concatenate([zeros, top], axis=0).astype(jnp.bfloat16)
    xb_ref[HALO:HALO + tm, :] = x_ref[0].astype(jnp.bfloat16)
    xb_ref[HALO + tm:, :] = jnp.concatenate([bot, zeros], axis=0).astype(jnp.bfloat16)
    xb = xb_ref[...]

    def conv(h_ref, cols):
        return (h_ref[HALO - 1:HALO - 1 + tm, :] * cw_ref[0:1, cols] + h_ref[HALO:HALO + tm, :] * cw_ref[1:2, cols]
                + h_ref[HALO + 1:HALO + 1 + tm, :] * cw_ref[2:3, cols] + cb_ref[:, cols])

    for f in range(d_ff // FF_BLOCK):
        cols_a = slice(f * FF_BLOCK, (f + 1) * FF_BLOCK)
        cols_g = slice(d_ff + f * FF_BLOCK, d_ff + (f + 1) * FF_BLOCK)
        ha_ref[...] = jnp.dot(xb, wup_ref[:, cols_a], preferred_element_type=jnp.float32)
        hg_ref[...] = jnp.dot(xb, wup_ref[:, cols_g], preferred_element_type=jnp.float32)
        a = conv(ha_ref, cols_a)
        gt = conv(hg_ref, cols_g)
        act_ref[:, cols_a] = (gt * (1.0 / (1.0 + jnp.exp(-gt))) * a).astype(jnp.bfloat16)

    f_out = jnp.dot(act_ref[...], wd_ref[...], preferred_element_type=jnp.float32)
    out_ref[0] = _layer_norm(DEEPNORM_ALPHA * x_ref[0] + f_out, g2_ref[...], b2_ref[...])


def _mlp(x, w_up, conv_w, conv_b, w_down, g2, b2):
    B, T, D = x.shape
    tm = TOK_TILE
    d_ff = w_down.shape[0]
    hb = tm // SUBLANES_F32
    last_hb = T // SUBLANES_F32 - 1
    tok = pl.BlockSpec((1, tm, D), lambda b_, t: (b_, t, 0))
    top = pl.BlockSpec((1, SUBLANES_F32, D), lambda b_, t: (b_, jnp.maximum(t * hb - 1, 0), 0))
    bot = pl.BlockSpec((1, SUBLANES_F32, D), lambda b_, t: (b_, jnp.minimum((t + 1) * hb, last_hb), 0))
    full = lambda a: pl.BlockSpec(a.shape, lambda b_, t: (0,) * a.ndim)
    resident = lambda a: pl.BlockSpec(a.shape, lambda b_, t: (0,) * a.ndim, pipeline_mode=pl.Buffered(1))
    return pl.pallas_call(
        _mlp_kernel,
        grid=(B, T // tm),
        in_specs=[tok, top, bot, resident(w_up), full(conv_w), full(conv_b), resident(w_down), full(g2), full(b2)],
        out_specs=tok,
        out_shape=jax.ShapeDtypeStruct((B, T, D), jnp.float32),
        scratch_shapes=[pltpu.VMEM((tm + 2 * HALO, D), jnp.bfloat16), pltpu.VMEM((tm, d_ff), jnp.bfloat16),
                        pltpu.VMEM((tm + 2 * HALO, FF_BLOCK), jnp.float32),
                        pltpu.VMEM((tm + 2 * HALO, FF_BLOCK), jnp.float32)],
        compiler_params=pltpu.CompilerParams(dimension_semantics=("parallel", "parallel"),
                                             vmem_limit_bytes=MLP_VMEM_LIMIT_BYTES),
        name="mlp",
    )(x, x, x, w_up, conv_w, conv_b, w_down, g2, b2)


def _prepare_weights(ln_in_g, ln_in_b, w_in, gw_f, gb_f, gw_b, gb_b, gla_norm_g, na_rpb, w_out,
                     ln1_g, ln1_b, w_up, conv_w, conv_b, w_down, ln2_g, ln2_b):
    bf = jnp.bfloat16
    row = lambda a: a.reshape(1, -1).astype(jnp.float32)
    w = w_in[0]
    sizes = (256, 256, 512, 512, 32, 512, 512, 512)
    offs = np.concatenate([[0], np.cumsum(sizes)])
    gq, gk, gv, gg, glr, nq, nk, nv = [w[:, offs[i]:offs[i + 1]] for i in range(8)]
    lr_pad = jnp.zeros((w.shape[0], LANES - 2 * GLA_LOWRANK), w.dtype)
    wa = jnp.concatenate([gq * (GLA_DK ** -0.5), gk, gg, glr, lr_pad, nq * (NA_DH ** -0.5), nv], axis=1).astype(bf)
    wt = jnp.concatenate([gv, nk], axis=1).T.astype(bf)
    zpad = jnp.zeros((LANES - 2 * GLA_LOWRANK, gw_f.shape[-1]), jnp.float32)
    zlr = jnp.zeros((GLA_LOWRANK, gw_f.shape[-1]), jnp.float32)
    wg_f = jnp.concatenate([gw_f[0], zlr, zpad], axis=0).astype(bf)
    wg_b = jnp.concatenate([zlr, gw_b[0], zpad], axis=0).astype(bf)
    bias = _na_bias_table(na_rpb[0])
    return dict(
        ln_in=(row(ln_in_g), row(ln_in_b)), wa=wa, wt=wt,
        wg_f=wg_f, bg_f=row(gb_f[0]), wg_b=wg_b, bg_b=row(gb_b[0]), ng=row(gla_norm_g[0]),
        bias=bias, w_out=w_out[0].astype(bf), ln1=(row(ln1_g[0]), row(ln1_b[0])),
        w_up=w_up[0].astype(bf), conv_w=conv_w[0].astype(jnp.float32), conv_b=row(conv_b[0]),
        w_down=w_down[0].astype(bf), ln2=(row(ln2_g[0]), row(ln2_b[0])),
    )


def _trunk(x, p):
    gq, gk, gg, lr, nq, nv, gvt, nkt = _in_proj(x, *p["ln_in"], p["wa"], p["wt"])
    o_bwd = _gla(gq, gk, gvt, lr, p["wg_b"], p["bg_b"], reverse=True)
    o_gla = _gla(gq, gk, gvt, lr, p["wg_f"], p["bg_f"], reverse=False, gg=gg, ob=o_bwd, ng=p["ng"])
    o_na = _natten(nq, nkt, nv, p["bias"])
    x1 = _out_proj(x, o_gla, o_na, p["w_out"], *p["ln_in"], *p["ln1"])
    return _mlp(x1, p["w_up"], p["conv_w"], p["conv_b"], p["w_down"], *p["ln2"])


def kernel(x_prompt, x_sample, ln_in_g, ln_in_b, w_in, gla_gate_w_fwd, gla_gate_b_fwd, gla_gate_w_bwd,
           gla_gate_b_bwd, gla_norm_g, na_rpb, w_out, ln1_g, ln1_b, w_up, conv_w, conv_b, w_down, ln2_g, ln2_b):
    p = _prepare_weights(ln_in_g, ln_in_b, w_in, gla_gate_w_fwd, gla_gate_b_fwd, gla_gate_w_bwd,
                         gla_gate_b_bwd, gla_norm_g, na_rpb, w_out, ln1_g, ln1_b, w_up, conv_w, conv_b,
                         w_down, ln2_g, ln2_b)
    return (_trunk(x_prompt, p), _trunk(x_sample, p))
```

```python
import functools

import numpy as np
import jax
import jax.numpy as jnp
from jax import lax
from jax.experimental import pallas as pl
from jax.experimental.pallas import tpu as pltpu

GRID_W = 64
GLA_HEADS, GLA_DK, GLA_DV, GLA_LOWRANK, GLA_TAU, GLA_CHUNK = 4, 64, 128, 16, 16.0, 64
NA_HEADS, NA_DH, NA_WIN_R, NA_WIN_C = 8, 64, 8, 16
LN_EPS, RMS_EPS = 1e-5, 1e-6
DEPTH = 1
DEEPNORM_ALPHA = (2.0 * DEPTH) ** 0.25

LANES = 128
SUBLANES_F32 = 8
SUBLANES_BF16 = 16
VMEM_LIMIT_BYTES = 48 * 1024 * 1024
MLP_VMEM_LIMIT_BYTES = 56 * 1024 * 1024

TOK_TILE = 512
GLA_BLOCK = 512
SUPER = 2 * GLA_CHUNK
NA_QROWS = 2
NA_WBLOCKS = 5
FF_BLOCK = 256
HALO = 16

_NT = (((1,), (1,)), ((), ()))


def _layer_norm(x, g, b):
    mu = jnp.mean(x, axis=-1, keepdims=True)
    xc = x - mu
    var = jnp.mean(xc * xc, axis=-1, keepdims=True)
    return xc * lax.rsqrt(var + LN_EPS) * g + b


def _cparams(sem):
    return pltpu.CompilerParams(dimension_semantics=sem, vmem_limit_bytes=VMEM_LIMIT_BYTES)


_A_GQ, _A_GK, _A_GG, _A_LR, _A_NQ, _A_NV, _A_END = 0, 256, 512, 1024, 1152, 1664, 2176


def _in_proj_kernel(x_ref, g_ref, b_ref, wa_ref, wt_ref,
                    gq_ref, gk_ref, gg_ref, lr_ref, nq_ref, nv_ref, gvt_ref, nkt_ref):
    xn = _layer_norm(x_ref[0], g_ref[...], b_ref[...]).astype(jnp.bfloat16)

    def proj(lo, hi, out_ref):
        out_ref[0] = jnp.dot(xn, wa_ref[:, lo:hi], preferred_element_type=jnp.float32).astype(out_ref.dtype)

    proj(_A_GQ, _A_GK, gq_ref)
    proj(_A_GK, _A_GG, gk_ref)
    proj(_A_GG, _A_LR, gg_ref)
    proj(_A_LR, _A_NQ, lr_ref)
    proj(_A_NQ, _A_NV, nq_ref)
    proj(_A_NV, _A_END, nv_ref)
    gvt_ref[0] = lax.dot_general(wt_ref[0:512, :], xn, _NT,
                                 preferred_element_type=jnp.float32).astype(gvt_ref.dtype)
    nkt_ref[0] = lax.dot_general(wt_ref[512:1024, :], xn, _NT,
                                 preferred_element_type=jnp.float32).astype(nkt_ref.dtype)


def _in_proj(x, g, b, wa, wt):
    B, T, D = x.shape
    tm = TOK_TILE
    tok = lambda n: pl.BlockSpec((1, tm, n), lambda bi, ti: (bi, ti, 0))
    tr = pl.BlockSpec((1, 512, tm), lambda bi, ti: (bi, 0, ti))
    full = lambda a: pl.BlockSpec(a.shape, lambda bi, ti: (0,) * a.ndim)
    bf = jnp.bfloat16
    out_shape = (
        jax.ShapeDtypeStruct((B, T, 256), bf), jax.ShapeDtypeStruct((B, T, 256), bf),
        jax.ShapeDtypeStruct((B, T, 512), bf), jax.ShapeDtypeStruct((B, T, 128), bf),
        jax.ShapeDtypeStruct((B, T, 512), bf), jax.ShapeDtypeStruct((B, T, 512), bf),
        jax.ShapeDtypeStruct((B, 512, T), bf), jax.ShapeDtypeStruct((B, 512, T), bf),
    )
    return pl.pallas_call(
        _in_proj_kernel,
        grid=(B, T // tm),
        in_specs=[tok(D), full(g), full(b), full(wa), full(wt)],
        out_specs=(tok(256), tok(256), tok(512), tok(128), tok(512), tok(512), tr, tr),
        out_shape=out_shape,
        compiler_params=_cparams(("parallel", "parallel")),
        name="in_proj",
    )(x, g, b, wa, wt)


def _log_sigmoid(z):
    return jnp.minimum(z, 0.0) - jnp.log1p(jnp.exp(-jnp.abs(z)))


def _gla_kernel(*refs, reverse, final):
    if final:
        (q_ref, k_ref, vt_ref, lr_ref, wg_ref, bg_ref, tri_ref, gg_ref, ob_ref, ng_ref,
         out_ref, st_ref) = refs
    else:
        (q_ref, k_ref, vt_ref, lr_ref, wg_ref, bg_ref, tri_ref, out_ref, st_ref) = refs
    C = GLA_CHUNK
    tb = q_ref.shape[1]
    nch, nsc = tb // C, tb // SUPER
    bf = jnp.bfloat16

    @pl.when(pl.program_id(1) == 0)
    def _():
        st_ref[...] = jnp.zeros_like(st_ref)

    z = jnp.dot(lr_ref[0], wg_ref[...], preferred_element_type=jnp.float32) + bg_ref[...]
    la = _log_sigmoid(z) * (1.0 / GLA_TAU)
    la_hi = la.astype(bf)
    la_lo = (la - la_hi.astype(jnp.float32)).astype(bf)
    tri = tri_ref[...]
    bcum = (jnp.dot(tri, la_hi, preferred_element_type=jnp.float32)
            + jnp.dot(tri, la_lo, preferred_element_type=jnp.float32))
    tot_row = 0 if reverse else C - 1
    dec = [jnp.exp(bcum[c * C + tot_row:c * C + tot_row + 1, :]) for c in range(nch)]
    q_in = (q_ref[0].astype(jnp.float32) * jnp.exp(bcum)).astype(bf)
    k_in32 = k_ref[0].astype(jnp.float32) * jnp.exp(-bcum)
    k_in = k_in32.astype(bf)
    k_end = jnp.concatenate([(k_in32[c * C:(c + 1) * C] * dec[c]).astype(bf) for c in range(nch)], axis=0)

    lane = lax.broadcasted_iota(jnp.int32, (SUPER, LANES), 1)
    row = lax.broadcasted_iota(jnp.int32, (SUPER, LANES), 0)
    head_lane = [lane < GLA_DK, lane >= GLA_DK]
    same_chunk = (row < C) == (lane < C)
    amask = same_chunk & ((lane >= row) if reverse else (lane <= row))
    zc = jnp.zeros((C, LANES), bf)

    qm, a_sc, ut = {}, {}, {}
    for sc in range(nsc):
        rows = slice(sc * SUPER, (sc + 1) * SUPER)
        for g in range(2):
            lanes = slice(g * LANES, (g + 1) * LANES)
            kin_sc, kend_sc = k_in[rows, lanes], k_end[rows, lanes]
            rhs = jnp.concatenate([jnp.concatenate([kend_sc[0:C], zc], axis=0),
                                   jnp.concatenate([zc, kend_sc[C:SUPER]], axis=0)], axis=1)
            ut_all = jnp.dot(vt_ref[0, g * 2 * GLA_DV:(g + 1) * 2 * GLA_DV, rows], rhs,
                             preferred_element_type=jnp.float32)
            for cc in range(2):
                cl = slice(cc * LANES, (cc + 1) * LANES)
                ut[2 * sc + cc, g] = jnp.where(head_lane[0], ut_all[0:GLA_DV, cl], ut_all[GLA_DV:2 * GLA_DV, cl])
            for hh in range(2):
                h = 2 * g + hh
                qp = q_in[rows, lanes]
                qm[sc, h] = jnp.where(head_lane[hh], qp, jnp.zeros_like(qp))
                a = lax.dot_general(qm[sc, h], kin_sc, _NT, preferred_element_type=jnp.float32)
                a_sc[sc, h] = jnp.where(amask, a, 0.0).astype(bf)

    st = [st_ref[0], st_ref[1]]
    st_at = {}
    for c in (range(nch - 1, -1, -1) if reverse else range(nch)):
        for g in range(2):
            st_at[c, g] = st[g].astype(bf)
            st[g] = st[g] * dec[c][:, g * LANES:(g + 1) * LANES] + ut[c, g]
    st_ref[0] = st[0]
    st_ref[1] = st[1]

    for sc in range(nsc):
        rows = slice(sc * SUPER, (sc + 1) * SUPER)
        for h in range(GLA_HEADS):
            g = h // 2
            q2 = qm[sc, h]
            lhs = jnp.concatenate([jnp.concatenate([q2[0:C], zc], axis=1),
                                   jnp.concatenate([zc, q2[C:SUPER]], axis=1)], axis=0)
            lhs = jnp.concatenate([lhs, a_sc[sc, h]], axis=1)
            rhs = jnp.concatenate([st_at[2 * sc, g], st_at[2 * sc + 1, g],
                                   vt_ref[0, h * GLA_DV:(h + 1) * GLA_DV, rows]], axis=1)
            o = lax.dot_general(lhs, rhs, _NT, preferred_element_type=jnp.float32)
            cols = slice(h * GLA_DV, (h + 1) * GLA_DV)
            if final:
                o = o + ob_ref[0, rows, cols]
                o = o * lax.rsqrt(jnp.mean(o * o, axis=-1, keepdims=True) + RMS_EPS) * ng_ref[...]
                gate = gg_ref[0, rows, cols].astype(jnp.float32)
                o = o * (gate * (1.0 / (1.0 + jnp.exp(-gate))))
            out_ref[0, rows, cols] = o.astype(out_ref.dtype)


def _gla(q, k, vt, lr, wg, bg, tri, *, reverse, gg=None, ob=None, ng=None):
    B, T, _ = q.shape
    tb = GLA_BLOCK
    nb = T // tb
    final = gg is not None
    blk = (lambda bi, i: (bi, nb - 1 - i, 0)) if reverse else (lambda bi, i: (bi, i, 0))
    blk_t = (lambda bi, i: (bi, 0, nb - 1 - i)) if reverse else (lambda bi, i: (bi, 0, i))
    tok = lambda n: pl.BlockSpec((1, tb, n), blk)
    full = lambda a: pl.BlockSpec(a.shape, lambda bi, i: (0,) * a.ndim)
    in_specs = [tok(256), tok(256), pl.BlockSpec((1, 512, tb), blk_t), tok(128), full(wg), full(bg), full(tri)]
    args = [q, k, vt, lr, wg, bg, tri]
    if final:
        in_specs += [tok(512), tok(512), full(ng)]
        args += [gg, ob, ng]
    return pl.pallas_call(
        functools.partial(_gla_kernel, reverse=reverse, final=final),
        grid=(B, nb),
        in_specs=in_specs,
        out_specs=tok(512),
        out_shape=jax.ShapeDtypeStruct((B, T, 512), jnp.bfloat16 if final else jnp.float32),
        scratch_shapes=[pltpu.VMEM((2, GLA_DV, 2 * GLA_DK), jnp.float32)],
        compiler_params=_cparams(("parallel", "arbitrary")),
        name="gla_fwd" if final else "gla_bwd",
    )(*args)


def _cumsum_operator(reverse):
    i = np.arange(GLA_BLOCK)
    same = (i[:, None] // GLA_CHUNK) == (i[None, :] // GLA_CHUNK)
    tri = (i[None, :] >= i[:, None]) if reverse else (i[None, :] <= i[:, None])
    return jnp.asarray(same & tri, jnp.bfloat16)


def _na_window_block(p, npairs):
    return jnp.clip(p - 2, 0, npairs - NA_WBLOCKS)


def _na_pair_type(p, npairs):
    return jnp.where(p < 2, p, jnp.where(p >= npairs - 2, p - (npairs - 5), 2))


def _na_bias_table(rpb):
    rows = 64
    npairs = rows // 2
    wk = NA_WBLOCKS * NA_QROWS
    nh = rpb.shape[0]
    ring = jnp.concatenate([rpb[..., NA_WIN_C - 1:], jnp.zeros(rpb.shape[:2] + (LANES - (2 * NA_WIN_C - 1),), rpb.dtype),
                            rpb[..., :NA_WIN_C - 1]], axis=-1)
    toep = jnp.tile(ring, (1, 1, GRID_W))[..., :GRID_W * (LANES - 1)]
    toep = toep.reshape(nh, rpb.shape[1], GRID_W, LANES - 1)[..., :GRID_W]
    c = np.arange(GRID_W)
    qcs = np.clip(c - NA_WIN_C // 2, 0, GRID_W - NA_WIN_C)
    col_ok = (c[None, :] >= qcs[:, None]) & (c[None, :] < qcs[:, None] + NA_WIN_C)
    toep = jnp.where(col_ok[None, None], toep, -jnp.inf)
    masked = jnp.full((nh, GRID_W, GRID_W), -jnp.inf, rpb.dtype)
    types = []
    for p in (0, 1, 2, npairs - 2, npairs - 1):
        w0 = 2 * int(np.clip(p - 2, 0, npairs - NA_WBLOCKS))
        qrows = []
        for a in range(NA_QROWS):
            r = 2 * p + a
            rs = int(np.clip(r - NA_WIN_R // 2, 0, rows - NA_WIN_R))
            blocks = [toep[:, w0 + i - r + NA_WIN_R - 1] if rs <= w0 + i < rs + NA_WIN_R else masked
                      for i in range(wk)]
            qrows.append(jnp.concatenate(blocks, axis=-1))
        types.append(jnp.concatenate(qrows, axis=-2))
    return jnp.stack(types, axis=0).astype(jnp.float32)


def _natten_kernel(*refs):
    q_ref = refs[0]
    kt_refs = refs[1:1 + NA_WBLOCKS]
    v_refs = refs[1 + NA_WBLOCKS:1 + 2 * NA_WBLOCKS]
    bias_ref = refs[1 + 2 * NA_WBLOCKS]
    out_ref = refs[2 + 2 * NA_WBLOCKS]
    nq = q_ref.shape[1]
    nk = nq * NA_WBLOCKS
    lane = lax.broadcasted_iota(jnp.int32, (nq, LANES), 1)
    head_lane = [lane < NA_DH, lane >= NA_DH]
    scores = []
    for g in range(NA_HEADS // 2):
        lanes = slice(g * LANES, (g + 1) * LANES)
        qp = q_ref[0, :, lanes]
        kt = jnp.concatenate([r[0, lanes, :] for r in kt_refs], axis=1)
        for hh in range(2):
            qm = jnp.where(head_lane[hh], qp, jnp.zeros_like(qp))
            scores.append(jnp.dot(qm, kt, preferred_element_type=jnp.float32) + bias_ref[0, 2 * g + hh])
    probs = []
    for s in scores:
        m = jnp.max(s, axis=-1, keepdims=True)
        probs.append(jnp.exp(s - m).astype(jnp.bfloat16))
    ones = jnp.ones((nk, LANES), jnp.bfloat16)
    for g in range(NA_HEADS // 2):
        lanes = slice(g * LANES, (g + 1) * LANES)
        v = jnp.concatenate([r[0, :, lanes] for r in v_refs], axis=0)
        v1 = jnp.concatenate([v, ones], axis=1)
        outs = []
        for hh in range(2):
            pv = jnp.dot(probs[2 * g + hh], v1, preferred_element_type=jnp.float32)
            outs.append(pv[:, 0:LANES] * (1.0 / pv[:, LANES:2 * LANES]))
        out_ref[0, :, lanes] = jnp.where(head_lane[0], outs[0], outs[1]).astype(out_ref.dtype)


def _natten(q, kt, v, bias):
    B, T, _ = q.shape
    nq = NA_QROWS * GRID_W
    npairs = T // nq
    q_spec = pl.BlockSpec((1, nq, 512), lambda bi, p: (bi, p, 0))
    kt_specs = [pl.BlockSpec((1, 512, nq), functools.partial(
        lambda bi, p, j: (bi, 0, _na_window_block(p, npairs) + j), j=j)) for j in range(NA_WBLOCKS)]
    v_specs = [pl.BlockSpec((1, nq, 512), functools.partial(
        lambda bi, p, j: (bi, _na_window_block(p, npairs) + j, 0), j=j)) for j in range(NA_WBLOCKS)]
    bias_spec = pl.BlockSpec((1,) + bias.shape[1:], lambda bi, p: (_na_pair_type(p, npairs), 0, 0, 0))
    return pl.pallas_call(
        _natten_kernel,
        grid=(B, npairs),
        in_specs=[q_spec] + kt_specs + v_specs + [bias_spec],
        out_specs=q_spec,
        out_shape=jax.ShapeDtypeStruct((B, T, 512), jnp.bfloat16),
        compiler_params=_cparams(("parallel", "arbitrary")),
        name="natten",
    )(q, *([kt] * NA_WBLOCKS), *([v] * NA_WBLOCKS), bias)


def _out_proj_kernel(x_ref, og_ref, on_ref, w_ref, gi_ref, bi_ref, g1_ref, b1_ref, out_ref):
    xn = _layer_norm(x_ref[0], gi_ref[...], bi_ref[...])
    mix = (jnp.dot(og_ref[0], w_ref[0:512, :], preferred_element_type=jnp.float32)
           + jnp.dot(on_ref[0], w_ref[512:1024, :], preferred_element_type=jnp.float32))
    out_ref[0] = _layer_norm(DEEPNORM_ALPHA * xn + mix, g1_ref[...], b1_ref[...])


def _out_proj(x, og, on, w, gi, bi, g1, b1):
    B, T, D = x.shape
    tm = TOK_TILE
    tok = lambda n: pl.BlockSpec((1, tm, n), lambda b_, t: (b_, t, 0))
    full = lambda a: pl.BlockSpec(a.shape, lambda b_, t: (0,) * a.ndim)
    return pl.pallas_call(
        _out_proj_kernel,
        grid=(B, T // tm),
        in_specs=[tok(D), tok(512), tok(512), full(w), full(gi), full(bi), full(g1), full(b1)],
        out_specs=tok(D),
        out_shape=jax.ShapeDtypeStruct((B, T, D), jnp.float32),
        compiler_params=_cparams(("parallel", "parallel")),
        name="out_proj",
    )(x, og, on, w, gi, bi, g1, b1)


def _mlp_kernel(x_ref, top_ref, bot_ref, wup_ref, cw_ref, cb_ref, wd_ref, g2_ref, b2_ref, out_ref,
                xb_ref, act_ref):
    t = pl.program_id(1)
    tm = x_ref.shape[1]
    d_ff = wd_ref.shape[0]
    rows = xb_ref.shape[0]

    zeros = jnp.zeros((HALO - SUBLANES_F32, x_ref.shape[2]), jnp.float32)
    top = jnp.where(t > 0, top_ref[0], 0.0)
    bot = jnp.where(t < pl.num_programs(1) - 1, bot_ref[0], 0.0)
    xb_ref[0:HALO, :] = jnp.concatenate([zeros, top], axis=0).astype(jnp.bfloat16)
    xb_ref[HALO:HALO + tm, :] = x_ref[0].astype(jnp.bfloat16)
    xb_ref[HALO + tm:, :] = jnp.concatenate([bot, zeros], axis=0).astype(jnp.bfloat16)
    xb = xb_ref[...]

    def conv(h, cols):
        prev = pltpu.roll(h, 1, 0)[HALO:HALO + tm]
        nxt = pltpu.roll(h, rows - 1, 0)[HALO:HALO + tm]
        return (prev * cw_ref[0:1, cols] + h[HALO:HALO + tm] * cw_ref[1:2, cols] + nxt * cw_ref[2:3, cols]
                + cb_ref[:, cols])

    for f in range(d_ff // FF_BLOCK):
        cols_a = slice(f * FF_BLOCK, (f + 1) * FF_BLOCK)
        cols_g = slice(d_ff + f * FF_BLOCK, d_ff + (f + 1) * FF_BLOCK)
        a = conv(jnp.dot(xb, wup_ref[:, cols_a], preferred_element_type=jnp.float32), cols_a)
        gt = conv(jnp.dot(xb, wup_ref[:, cols_g], preferred_element_type=jnp.float32), cols_g)
        act_ref[:, cols_a] = (gt * (1.0 / (1.0 + jnp.exp(-gt))) * a).astype(jnp.bfloat16)

    f_out = jnp.dot(act_ref[...], wd_ref[...], preferred_element_type=jnp.float32)
    out_ref[0] = _layer_norm(DEEPNORM_ALPHA * x_ref[0] + f_out, g2_ref[...], b2_ref[...])


def _mlp(x, w_up, conv_w, conv_b, w_down, g2, b2):
    B, T, D = x.shape
    tm = TOK_TILE
    d_ff = w_down.shape[0]
    hb = tm // SUBLANES_F32
    last_hb = T // SUBLANES_F32 - 1
    tok = pl.BlockSpec((1, tm, D), lambda b_, t: (b_, t, 0))
    top = pl.BlockSpec((1, SUBLANES_F32, D), lambda b_, t: (b_, jnp.maximum(t * hb - 1, 0), 0))
    bot = pl.BlockSpec((1, SUBLANES_F32, D), lambda b_, t: (b_, jnp.minimum((t + 1) * hb, last_hb), 0))
    full = lambda a: pl.BlockSpec(a.shape, lambda b_, t: (0,) * a.ndim)
    resident = lambda a: pl.BlockSpec(a.shape, lambda b_, t: (0,) * a.ndim, pipeline_mode=pl.Buffered(1))
    return pl.pallas_call(
        _mlp_kernel,
        grid=(B, T // tm),
        in_specs=[tok, top, bot, resident(w_up), full(conv_w), full(conv_b), resident(w_down), full(g2), full(b2)],
        out_specs=tok,
        out_shape=jax.ShapeDtypeStruct((B, T, D), jnp.float32),
        scratch_shapes=[pltpu.VMEM((tm + 2 * HALO, D), jnp.bfloat16), pltpu.VMEM((tm, d_ff), jnp.bfloat16)],
        compiler_params=pltpu.CompilerParams(dimension_semantics=("parallel", "parallel"),
                                             vmem_limit_bytes=MLP_VMEM_LIMIT_BYTES),
        name="mlp",
    )(x, x, x, w_up, conv_w, conv_b, w_down, g2, b2)


def _prepare_weights(ln_in_g, ln_in_b, w_in, gw_f, gb_f, gw_b, gb_b, gla_norm_g, na_rpb, w_out,
                     ln1_g, ln1_b, w_up, conv_w, conv_b, w_down, ln2_g, ln2_b):
    bf = jnp.bfloat16
    row = lambda a: a.reshape(1, -1).astype(jnp.float32)
    w = w_in[0]
    sizes = (256, 256, 512, 512, 32, 512, 512, 512)
    offs = np.concatenate([[0], np.cumsum(sizes)])
    gq, gk, gv, gg, glr, nq, nk, nv = [w[:, offs[i]:offs[i + 1]] for i in range(8)]
    lr_pad = jnp.zeros((w.shape[0], LANES - 2 * GLA_LOWRANK), w.dtype)
    wa = jnp.concatenate([gq * (GLA_DK ** -0.5), gk, gg, glr, lr_pad, nq * (NA_DH ** -0.5), nv], axis=1).astype(bf)
    wt = jnp.concatenate([gv, nk], axis=1).T.astype(bf)
    zpad = jnp.zeros((LANES - 2 * GLA_LOWRANK, gw_f.shape[-1]), jnp.float32)
    zlr = jnp.zeros((GLA_LOWRANK, gw_f.shape[-1]), jnp.float32)
    wg_f = jnp.concatenate([gw_f[0], zlr, zpad], axis=0).astype(bf)
    wg_b = jnp.concatenate([zlr, gw_b[0], zpad], axis=0).astype(bf)
    bias = _na_bias_table(na_rpb[0])
    return dict(
        ln_in=(row(ln_in_g), row(ln_in_b)), wa=wa, wt=wt,
        wg_f=wg_f, bg_f=row(gb_f[0]), wg_b=wg_b, bg_b=row(gb_b[0]), ng=row(gla_norm_g[0]),
        tri_f=_cumsum_operator(False), tri_b=_cumsum_operator(True),
        bias=bias, w_out=w_out[0].astype(bf), ln1=(row(ln1_g[0]), row(ln1_b[0])),
        w_up=w_up[0].astype(bf), conv_w=conv_w[0].astype(jnp.float32), conv_b=row(conv_b[0]),
        w_down=w_down[0].astype(bf), ln2=(row(ln2_g[0]), row(ln2_b[0])),
    )


def _trunk(x, p):
    gq, gk, gg, lr, nq, nv, gvt, nkt = _in_proj(x, *p["ln_in"], p["wa"], p["wt"])
    o_bwd = _gla(gq, gk, gvt, lr, p["wg_b"], p["bg_b"], p["tri_b"], reverse=True)
    o_gla = _gla(gq, gk, gvt, lr, p["wg_f"], p["bg_f"], p["tri_f"], reverse=False, gg=gg, ob=o_bwd, ng=p["ng"])
    o_na = _natten(nq, nkt, nv, p["bias"])
    x1 = _out_proj(x, o_gla, o_na, p["w_out"], *p["ln_in"], *p["ln1"])
    return _mlp(x1, p["w_up"], p["conv_w"], p["conv_b"], p["w_down"], *p["ln2"])


def kernel(x_prompt, x_sample, ln_in_g, ln_in_b, w_in, gla_gate_w_fwd, gla_gate_b_fwd, gla_gate_w_bwd,
           gla_gate_b_bwd, gla_norm_g, na_rpb, w_out, ln1_g, ln1_b, w_up, conv_w, conv_b, w_down, ln2_g, ln2_b):
    p = _prepare_weights(ln_in_g, ln_in_b, w_in, gla_gate_w_fwd, gla_gate_b_fwd, gla_gate_w_bwd,
                         gla_gate_b_bwd, gla_norm_g, na_rpb, w_out, ln1_g, ln1_b, w_up, conv_w, conv_b,
                         w_down, ln2_g, ln2_b)
    return (_trunk(x_prompt, p), _trunk(x_sample, p))
```

```python
import functools

import numpy as np
import jax
import jax.numpy as jnp
from jax import lax
from jax.experimental import pallas as pl
from jax.experimental.pallas import tpu as pltpu

GRID_W = 64
GLA_HEADS, GLA_DK, GLA_DV, GLA_LOWRANK, GLA_TAU, GLA_CHUNK = 4, 64, 128, 16, 16.0, 64
NA_HEADS, NA_DH, NA_WIN_R, NA_WIN_C = 8, 64, 8, 16
LN_EPS, RMS_EPS = 1e-5, 1e-6
DEPTH = 1
DEEPNORM_ALPHA = (2.0 * DEPTH) ** 0.25

LANES = 128
SUBLANES_F32 = 8
SUBLANES_BF16 = 16
VMEM_LIMIT_BYTES = 48 * 1024 * 1024
MLP_VMEM_LIMIT_BYTES = 56 * 1024 * 1024

TOK_TILE = 512
GLA_BLOCK = 1024
SUPER = 2 * GLA_CHUNK
GATE_AHEAD = 2
NA_QROWS = 2
NA_WBLOCKS = 5
NA_PAIRS = 4
FF_BLOCK = 256
HALO = SUBLANES_F32

_NT = (((1,), (1,)), ((), ()))


def _layer_norm(x, g, b):
    mu = jnp.mean(x, axis=-1, keepdims=True)
    xc = x - mu
    var = jnp.mean(xc * xc, axis=-1, keepdims=True)
    return xc * lax.rsqrt(var + LN_EPS) * g + b


def _cparams(sem, flags=None):
    return pltpu.CompilerParams(dimension_semantics=sem, vmem_limit_bytes=VMEM_LIMIT_BYTES, flags=flags)


SCHED_FLAGS = None


_A_GQ, _A_GK, _A_GG, _A_LR, _A_NQ, _A_NV, _A_END = 0, 256, 512, 1024, 1152, 1664, 2176


def _in_proj_kernel(x_ref, g_ref, b_ref, wa_ref, wt_ref,
                    gq_ref, gk_ref, gg_ref, lr_ref, nq_ref, nv_ref, gvt_ref, nkt_ref):
    xn = _layer_norm(x_ref[0], g_ref[...], b_ref[...]).astype(jnp.bfloat16)

    def proj(lo, hi, out_ref):
        out_ref[0] = jnp.dot(xn, wa_ref[:, lo:hi], preferred_element_type=jnp.float32).astype(out_ref.dtype)

    proj(_A_GQ, _A_GK, gq_ref)
    proj(_A_GK, _A_GG, gk_ref)
    proj(_A_GG, _A_LR, gg_ref)
    proj(_A_LR, _A_NQ, lr_ref)
    proj(_A_NQ, _A_NV, nq_ref)
    proj(_A_NV, _A_END, nv_ref)
    gvt_ref[0] = lax.dot_general(wt_ref[0:512, :], xn, _NT,
                                 preferred_element_type=jnp.float32).astype(gvt_ref.dtype)
    nkt_ref[0] = lax.dot_general(wt_ref[512:1024, :], xn, _NT,
                                 preferred_element_type=jnp.float32).astype(nkt_ref.dtype)


def _in_proj(x, g, b, wa, wt):
    B, T, D = x.shape
    tm = TOK_TILE
    tok = lambda n: pl.BlockSpec((1, tm, n), lambda bi, ti: (bi, ti, 0))
    tr = pl.BlockSpec((1, 512, tm), lambda bi, ti: (bi, 0, ti))
    full = lambda a: pl.BlockSpec(a.shape, lambda bi, ti: (0,) * a.ndim)
    bf = jnp.bfloat16
    out_shape = (
        jax.ShapeDtypeStruct((B, T, 256), bf), jax.ShapeDtypeStruct((B, T, 256), bf),
        jax.ShapeDtypeStruct((B, T, 512), bf), jax.ShapeDtypeStruct((B, T, 128), bf),
        jax.ShapeDtypeStruct((B, T, 512), bf), jax.ShapeDtypeStruct((B, T, 512), bf),
        jax.ShapeDtypeStruct((B, 512, T), bf), jax.ShapeDtypeStruct((B, 512, T), bf),
    )
    return pl.pallas_call(
        _in_proj_kernel,
        grid=(B, T // tm),
        in_specs=[tok(D), full(g), full(b), full(wa), full(wt)],
        out_specs=(tok(256), tok(256), tok(512), tok(128), tok(512), tok(512), tr, tr),
        out_shape=out_shape,
        compiler_params=_cparams(("parallel", "parallel")),
        name="in_proj",
    )(x, g, b, wa, wt)


def _log_sigmoid(z):
    return jnp.minimum(z, 0.0) - jnp.log(1.0 + jnp.exp(-jnp.abs(z)))


def _gla_kernel(*refs, reverse, final):
    if final:
        (q_ref, k_ref, vt_ref, lr_ref, wg_ref, bg_ref, tri_ref, gg_ref, ob_ref, ng_ref,
         out_ref, st_ref) = refs
    else:
        (q_ref, k_ref, vt_ref, lr_ref, wg_ref, bg_ref, tri_ref, out_ref, st_ref) = refs
    C = GLA_CHUNK
    tb = q_ref.shape[1]
    nch, nsc = tb // C, tb // SUPER
    bf = jnp.bfloat16

    @pl.when(pl.program_id(1) == 0)
    def _():
        st_ref[...] = jnp.zeros_like(st_ref)

    tri = tri_ref[...]
    la_parts, bcums = {}, {}

    def gate_logits(sc):
        rows = slice(sc * SUPER, (sc + 1) * SUPER)
        z = jnp.dot(lr_ref[0, rows, :], wg_ref[...], preferred_element_type=jnp.float32) + bg_ref[...]
        la = _log_sigmoid(z) * (1.0 / GLA_TAU)
        la_hi = la.astype(bf)
        la_parts[sc] = (la_hi, (la - la_hi.astype(jnp.float32)).astype(bf))

    def gate_cumsum(sc):
        la_hi, la_lo = la_parts.pop(sc)
        bcums[sc] = (jnp.dot(tri, la_hi, preferred_element_type=jnp.float32)
                     + jnp.dot(tri, la_lo, preferred_element_type=jnp.float32))

    tot_row = 0 if reverse else C - 1

    lane = lax.broadcasted_iota(jnp.int32, (SUPER, LANES), 1)
    row = lax.broadcasted_iota(jnp.int32, (SUPER, LANES), 0)
    head_lane = [lane < GLA_DK, lane >= GLA_DK]
    same_chunk = (row < C) == (lane < C)
    amask = same_chunk & ((lane >= row) if reverse else (lane <= row))
    zc = jnp.zeros((C, LANES), bf)

    qm, a_sc, ut, dec, st_at = {}, {}, {}, {}, {}
    st = [st_ref[0], st_ref[1]]

    def products(sc):
        rows = slice(sc * SUPER, (sc + 1) * SUPER)
        bcum = bcums.pop(sc)
        for cc in range(2):
            dec[2 * sc + cc] = jnp.exp(bcum[cc * C + tot_row:cc * C + tot_row + 1, :])
        q_in = (q_ref[0, rows, :].astype(jnp.float32) * jnp.exp(bcum)).astype(bf)
        k_in32 = k_ref[0, rows, :].astype(jnp.float32) * jnp.exp(-bcum)
        k_in = k_in32.astype(bf)
        k_end = jnp.concatenate([(k_in32[cc * C:(cc + 1) * C] * dec[2 * sc + cc]).astype(bf) for cc in range(2)],
                                axis=0)
        for g in range(2):
            lanes = slice(g * LANES, (g + 1) * LANES)
            kin_sc, kend_sc = k_in[:, lanes], k_end[:, lanes]
            rhs = jnp.concatenate([jnp.concatenate([kend_sc[0:C], zc], axis=0),
                                   jnp.concatenate([zc, kend_sc[C:SUPER]], axis=0)], axis=1)
            ut_all = jnp.dot(vt_ref[0, g * 2 * GLA_DV:(g + 1) * 2 * GLA_DV, rows], rhs,
                             preferred_element_type=jnp.float32)
            for cc in range(2):
                cl = slice(cc * LANES, (cc + 1) * LANES)
                ut[2 * sc + cc, g] = jnp.where(head_lane[0], ut_all[0:GLA_DV, cl], ut_all[GLA_DV:2 * GLA_DV, cl])
            for hh in range(2):
                h = 2 * g + hh
                qp = q_in[:, lanes]
                qm[sc, h] = jnp.where(head_lane[hh], qp, jnp.zeros_like(qp))
                a = lax.dot_general(qm[sc, h], kin_sc, _NT, preferred_element_type=jnp.float32)
                a_sc[sc, h] = jnp.where(amask, a, 0.0).astype(bf)

    def advance_state(sc):
        for c in ((2 * sc + 1, 2 * sc) if reverse else (2 * sc, 2 * sc + 1)):
            for g in range(2):
                st_at[c, g] = st[g].astype(bf)
                st[g] = st[g] * dec[c][:, g * LANES:(g + 1) * LANES] + ut.pop((c, g))

    def outputs(sc):
        rows = slice(sc * SUPER, (sc + 1) * SUPER)
        for h in range(GLA_HEADS):
            g = h // 2
            q2 = qm[sc, h]
            lhs = jnp.concatenate([jnp.concatenate([q2[0:C], zc], axis=1),
                                   jnp.concatenate([zc, q2[C:SUPER]], axis=1)], axis=0)
            lhs = jnp.concatenate([lhs, a_sc[sc, h]], axis=1)
            rhs = jnp.concatenate([st_at[2 * sc, g], st_at[2 * sc + 1, g],
                                   vt_ref[0, h * GLA_DV:(h + 1) * GLA_DV, rows]], axis=1)
            o = lax.dot_general(lhs, rhs, _NT, preferred_element_type=jnp.float32)
            cols = slice(h * GLA_DV, (h + 1) * GLA_DV)
            if final:
                o = o + ob_ref[0, rows, cols]
                o = o * lax.rsqrt(jnp.mean(o * o, axis=-1, keepdims=True) + RMS_EPS) * ng_ref[...]
                gate = gg_ref[0, rows, cols].astype(jnp.float32)
                o = o * (gate * (1.0 / (1.0 + jnp.exp(-gate))))
            out_ref[0, rows, cols] = o.astype(out_ref.dtype)

    order = list(range(nsc - 1, -1, -1) if reverse else range(nsc))
    ahead = nsc if final else GATE_AHEAD
    for sc in order[:ahead]:
        gate_logits(sc)
    for sc in order[:ahead - 1]:
        gate_cumsum(sc)
    for i, sc in enumerate(order):
        if i + ahead < nsc:
            gate_logits(order[i + ahead])
        if i + ahead - 1 < nsc:
            gate_cumsum(order[i + ahead - 1])
        products(sc)
        advance_state(sc)
        if i > 0:
            outputs(order[i - 1])
    outputs(order[-1])
    st_ref[0] = st[0]
    st_ref[1] = st[1]


def _gla(q, k, vt, lr, wg, bg, tri, *, reverse, gg=None, ob=None, ng=None):
    B, T, _ = q.shape
    tb = GLA_BLOCK
    nb = T // tb
    final = gg is not None
    blk = (lambda bi, i: (bi, nb - 1 - i, 0)) if reverse else (lambda bi, i: (bi, i, 0))
    blk_t = (lambda bi, i: (bi, 0, nb - 1 - i)) if reverse else (lambda bi, i: (bi, 0, i))
    tok = lambda n: pl.BlockSpec((1, tb, n), blk)
    full = lambda a: pl.BlockSpec(a.shape, lambda bi, i: (0,) * a.ndim)
    in_specs = [tok(256), tok(256), pl.BlockSpec((1, 512, tb), blk_t), tok(128), full(wg), full(bg), full(tri)]
    args = [q, k, vt, lr, wg, bg, tri]
    if final:
        in_specs += [tok(512), tok(512), full(ng)]
        args += [gg, ob, ng]
    return pl.pallas_call(
        functools.partial(_gla_kernel, reverse=reverse, final=final),
        grid=(B, nb),
        in_specs=in_specs,
        out_specs=tok(512),
        out_shape=jax.ShapeDtypeStruct((B, T, 512), jnp.bfloat16 if final else jnp.float32),
        scratch_shapes=[pltpu.VMEM((2, GLA_DV, 2 * GLA_DK), jnp.float32)],
        compiler_params=_cparams(("parallel", "arbitrary"), SCHED_FLAGS),
        name="gla_fwd" if final else "gla_bwd",
    )(*args)


def _cumsum_operator(reverse):
    i = np.arange(SUPER)
    same = (i[:, None] // GLA_CHUNK) == (i[None, :] // GLA_CHUNK)
    tri = (i[None, :] >= i[:, None]) if reverse else (i[None, :] <= i[:, None])
    return jnp.asarray(same & tri, jnp.bfloat16)


def _na_window_block(p, npairs):
    return jnp.clip(p - 2, 0, npairs - NA_WBLOCKS)


def _na_pair_type(p, npairs):
    return jnp.where(p < 2, p, jnp.where(p >= npairs - 2, p - (npairs - 5), 2))


def _na_bias_table(rpb):
    rows = 64
    npairs = rows // 2
    wk = NA_WBLOCKS * NA_QROWS
    nh = rpb.shape[0]
    ring = jnp.concatenate([rpb[..., NA_WIN_C - 1:], jnp.zeros(rpb.shape[:2] + (LANES - (2 * NA_WIN_C - 1),), rpb.dtype),
                            rpb[..., :NA_WIN_C - 1]], axis=-1)
    toep = jnp.tile(ring, (1, 1, GRID_W))[..., :GRID_W * (LANES - 1)]
    toep = toep.reshape(nh, rpb.shape[1], GRID_W, LANES - 1)[..., :GRID_W]
    c = np.arange(GRID_W)
    qcs = np.clip(c - NA_WIN_C // 2, 0, GRID_W - NA_WIN_C)
    col_ok = (c[None, :] >= qcs[:, None]) & (c[None, :] < qcs[:, None] + NA_WIN_C)
    toep = jnp.where(col_ok[None, None], toep, -jnp.inf)
    masked = jnp.full((nh, GRID_W, GRID_W), -jnp.inf, rpb.dtype)
    types = []
    for p in (0, 1, 2, npairs - 2, npairs - 1):
        w0 = 2 * int(np.clip(p - 2, 0, npairs - NA_WBLOCKS))
        qrows = []
        for a in range(NA_QROWS):
            r = 2 * p + a
            rs = int(np.clip(r - NA_WIN_R // 2, 0, rows - NA_WIN_R))
            blocks = [toep[:, w0 + i - r + NA_WIN_R - 1] if rs <= w0 + i < rs + NA_WIN_R else masked
                      for i in range(wk)]
            qrows.append(jnp.concatenate(blocks, axis=-1))
        types.append(jnp.concatenate(qrows, axis=-2))
    return jnp.stack(types, axis=0).astype(jnp.float32)


def _natten_kernel(*refs):
    q_ref, out_ref = refs[0], refs[-1]
    nq = NA_QROWS * GRID_W
    nk = nq * NA_WBLOCKS
    lane = lax.broadcasted_iota(jnp.int32, (nq, LANES), 1)
    head_lane = [lane < NA_DH, lane >= NA_DH]
    ones = jnp.ones((nk, LANES), jnp.bfloat16)

    def pair_refs(j):
        return refs[1 + 3 * j:4 + 3 * j]

    def qk(j, h):
        kt_ref, _, bias_ref = pair_refs(j)
        g, hh = divmod(h, 2)
        lanes = slice(g * LANES, (g + 1) * LANES)
        qp = q_ref[0, j * nq:(j + 1) * nq, lanes]
        qm = jnp.where(head_lane[hh], qp, jnp.zeros_like(qp))
        return jnp.dot(qm, kt_ref[lanes, :], preferred_element_type=jnp.float32) + bias_ref[0, h]

    def softmax_num(s):
        m = jnp.max(s, axis=-1, keepdims=True)
        return jnp.exp(s - m).astype(jnp.bfloat16)

    def pv(j, h, p):
        _, v_ref, _ = pair_refs(j)
        lanes = slice((h // 2) * LANES, (h // 2 + 1) * LANES)
        v1 = jnp.concatenate([v_ref[:, lanes], ones], axis=1)
        o = jnp.dot(p, v1, preferred_element_type=jnp.float32)
        return o[:, 0:LANES] * (1.0 / o[:, LANES:2 * LANES])

    scores = {(j, h): qk(j, h) for j in range(NA_PAIRS) for h in range(NA_HEADS)}
    for j in range(NA_PAIRS):
        for g in range(NA_HEADS // 2):
            o0 = pv(j, 2 * g, softmax_num(scores.pop((j, 2 * g))))
            o1 = pv(j, 2 * g + 1, softmax_num(scores.pop((j, 2 * g + 1))))
            out_ref[0, j * nq:(j + 1) * nq, g * LANES:(g + 1) * LANES] = \
                jnp.where(head_lane[0], o0, o1).astype(out_ref.dtype)


def _natten(q, kt, v, bias):
    B, T, _ = q.shape
    nq = NA_QROWS * GRID_W
    npairs = T // nq
    q_spec = pl.BlockSpec((1, NA_PAIRS * nq, 512), lambda bi, s: (bi, s, 0))
    in_specs, args = [q_spec], [q]
    nk = nq * NA_WBLOCKS
    for j in range(NA_PAIRS):
        pair = functools.partial(lambda s, j: NA_PAIRS * s + j, j=j)
        in_specs += [
            pl.BlockSpec((None, pl.Element(512), pl.Element(nk)), functools.partial(
                lambda bi, s, pair: (bi, 0, _na_window_block(pair(s), npairs) * nq), pair=pair)),
            pl.BlockSpec((None, pl.Element(nk), pl.Element(512)), functools.partial(
                lambda bi, s, pair: (bi, _na_window_block(pair(s), npairs) * nq, 0), pair=pair)),
            pl.BlockSpec((1,) + bias.shape[1:], functools.partial(
                lambda bi, s, pair: (_na_pair_type(pair(s), npairs), 0, 0, 0), pair=pair),
                pipeline_mode=pl.Buffered(1))]
        args += [kt, v, bias]
    return pl.pallas_call(
        _natten_kernel,
        grid=(B, npairs // NA_PAIRS),
        in_specs=in_specs,
        out_specs=q_spec,
        out_shape=jax.ShapeDtypeStruct((B, T, 512), jnp.bfloat16),
        compiler_params=_cparams(("parallel", "arbitrary"), SCHED_FLAGS),
        name="natten",
    )(*args)


def _out_proj_kernel(x_ref, og_ref, on_ref, w_ref, gi_ref, bi_ref, g1_ref, b1_ref, out_ref):
    xn = _layer_norm(x_ref[0], gi_ref[...], bi_ref[...])
    mix = (jnp.dot(og_ref[0], w_ref[0:512, :], preferred_element_type=jnp.float32)
           + jnp.dot(on_ref[0], w_ref[512:1024, :], preferred_element_type=jnp.float32))
    out_ref[0] = _layer_norm(DEEPNORM_ALPHA * xn + mix, g1_ref[...], b1_ref[...])


def _out_proj(x, og, on, w, gi, bi, g1, b1):
    B, T, D = x.shape
    tm = TOK_TILE
    tok = lambda n: pl.BlockSpec((1, tm, n), lambda b_, t: (b_, t, 0))
    full = lambda a: pl.BlockSpec(a.shape, lambda b_, t: (0,) * a.ndim)
    return pl.pallas_call(
        _out_proj_kernel,
        grid=(B, T // tm),
        in_specs=[tok(D), tok(512), tok(512), full(w), full(gi), full(bi), full(g1), full(b1)],
        out_specs=tok(D),
        out_shape=jax.ShapeDtypeStruct((B, T, D), jnp.float32),
        compiler_params=_cparams(("parallel", "parallel")),
        name="out_proj",
    )(x, og, on, w, gi, bi, g1, b1)


def _mlp_kernel(x_ref, top_ref, bot_ref, wup_ref, cw_ref, cb_ref, wd_ref, g2_ref, b2_ref, out_ref,
                xb_ref, act_ref):
    t = pl.program_id(1)
    tm = x_ref.shape[1]
    d_ff = wd_ref.shape[0]
    rows = xb_ref.shape[0]

    top = jnp.where(t > 0, top_ref[0], 0.0)
    bot = jnp.where(t < pl.num_programs(1) - 1, bot_ref[0], 0.0)
    xb_ref[...] = jnp.concatenate([top, x_ref[0], bot], axis=0).astype(jnp.bfloat16)
    xb = xb_ref[...]

    def conv(h, cols):
        prev = pltpu.roll(h, 1, 0)[HALO:HALO + tm]
        nxt = pltpu.roll(h, rows - 1, 0)[HALO:HALO + tm]
        return (prev * cw_ref[0:1, cols] + h[HALO:HALO + tm] * cw_ref[1:2, cols] + nxt * cw_ref[2:3, cols]
                + cb_ref[:, cols])

    for f in range(d_ff // FF_BLOCK):
        cols_a = slice(f * FF_BLOCK, (f + 1) * FF_BLOCK)
        cols_g = slice(d_ff + f * FF_BLOCK, d_ff + (f + 1) * FF_BLOCK)
        a = conv(jnp.dot(xb, wup_ref[:, cols_a], preferred_element_type=jnp.float32), cols_a)
        gt = conv(jnp.dot(xb, wup_ref[:, cols_g], preferred_element_type=jnp.float32), cols_g)
        act_ref[:, cols_a] = (gt * (1.0 / (1.0 + jnp.exp(-gt))) * a).astype(jnp.bfloat16)

    f_out = jnp.dot(act_ref[...], wd_ref[...], preferred_element_type=jnp.float32)
    out_ref[0] = _layer_norm(DEEPNORM_ALPHA * x_ref[0] + f_out, g2_ref[...], b2_ref[...])


def _mlp(x, w_up, conv_w, conv_b, w_down, g2, b2):
    B, T, D = x.shape
    tm = TOK_TILE
    d_ff = w_down.shape[0]
    hb = tm // SUBLANES_F32
    last_hb = T // SUBLANES_F32 - 1
    tok = pl.BlockSpec((1, tm, D), lambda b_, t: (b_, t, 0))
    top = pl.BlockSpec((1, SUBLANES_F32, D), lambda b_, t: (b_, jnp.maximum(t * hb - 1, 0), 0))
    bot = pl.BlockSpec((1, SUBLANES_F32, D), lambda b_, t: (b_, jnp.minimum((t + 1) * hb, last_hb), 0))
    full = lambda a: pl.BlockSpec(a.shape, lambda b_, t: (0,) * a.ndim)
    resident = lambda a: pl.BlockSpec(a.shape, lambda b_, t: (0,) * a.ndim, pipeline_mode=pl.Buffered(1))
    return pl.pallas_call(
        _mlp_kernel,
        grid=(B, T // tm),
        in_specs=[tok, top, bot, resident(w_up), full(conv_w), full(conv_b), resident(w_down), full(g2), full(b2)],
        out_specs=tok,
        out_shape=jax.ShapeDtypeStruct((B, T, D), jnp.float32),
        scratch_shapes=[pltpu.VMEM((tm + 2 * HALO, D), jnp.bfloat16), pltpu.VMEM((tm, d_ff), jnp.bfloat16)],
        compiler_params=pltpu.CompilerParams(dimension_semantics=("parallel", "parallel"),
                                             vmem_limit_bytes=MLP_VMEM_LIMIT_BYTES),
        name="mlp",
    )(x, x, x, w_up, conv_w, conv_b, w_down, g2, b2)


def _prepare_weights(ln_in_g, ln_in_b, w_in, gw_f, gb_f, gw_b, gb_b, gla_norm_g, na_rpb, w_out,
                     ln1_g, ln1_b, w_up, conv_w, conv_b, w_down, ln2_g, ln2_b):
    bf = jnp.bfloat16
    row = lambda a: a.reshape(1, -1).astype(jnp.float32)
    w = w_in[0]
    sizes = (256, 256, 512, 512, 32, 512, 512, 512)
    offs = np.concatenate([[0], np.cumsum(sizes)])
    gq, gk, gv, gg, glr, nq, nk, nv = [w[:, offs[i]:offs[i + 1]] for i in range(8)]
    lr_pad = jnp.zeros((w.shape[0], LANES - 2 * GLA_LOWRANK), w.dtype)
    wa = jnp.concatenate([gq * (GLA_DK ** -0.5), gk, gg, glr, lr_pad, nq * (NA_DH ** -0.5), nv], axis=1).astype(bf)
    wt = jnp.concatenate([gv, nk], axis=1).T.astype(bf)
    zpad = jnp.zeros((LANES - 2 * GLA_LOWRANK, gw_f.shape[-1]), jnp.float32)
    zlr = jnp.zeros((GLA_LOWRANK, gw_f.shape[-1]), jnp.float32)
    wg_f = jnp.concatenate([gw_f[0], zlr, zpad], axis=0).astype(bf)
    wg_b = jnp.concatenate([zlr, gw_b[0], zpad], axis=0).astype(bf)
    bias = _na_bias_table(na_rpb[0])
    return dict(
        ln_in=(row(ln_in_g), row(ln_in_b)), wa=wa, wt=wt,
        wg_f=wg_f, bg_f=row(gb_f[0]), wg_b=wg_b, bg_b=row(gb_b[0]), ng=row(gla_norm_g[0]),
        tri_f=_cumsum_operator(False), tri_b=_cumsum_operator(True),
        bias=bias, w_out=w_out[0].astype(bf), ln1=(row(ln1_g[0]), row(ln1_b[0])),
        w_up=w_up[0].astype(bf), conv_w=conv_w[0].astype(jnp.float32), conv_b=row(conv_b[0]),
        w_down=w_down[0].astype(bf), ln2=(row(ln2_g[0]), row(ln2_b[0])),
    )


def _trunk(x, p):
    gq, gk, gg, lr, nq, nv, gvt, nkt = _in_proj(x, *p["ln_in"], p["wa"], p["wt"])
    o_bwd = _gla(gq, gk, gvt, lr, p["wg_b"], p["bg_b"], p["tri_b"], reverse=True)
    o_gla = _gla(gq, gk, gvt, lr, p["wg_f"], p["bg_f"], p["tri_f"], reverse=False, gg=gg, ob=o_bwd, ng=p["ng"])
    o_na = _natten(nq, nkt, nv, p["bias"])
    x1 = _out_proj(x, o_gla, o_na, p["w_out"], *p["ln_in"], *p["ln1"])
    return _mlp(x1, p["w_up"], p["conv_w"], p["conv_b"], p["w_down"], *p["ln2"])


def kernel(x_prompt, x_sample, ln_in_g, ln_in_b, w_in, gla_gate_w_fwd, gla_gate_b_fwd, gla_gate_w_bwd,
           gla_gate_b_bwd, gla_norm_g, na_rpb, w_out, ln1_g, ln1_b, w_up, conv_w, conv_b, w_down, ln2_g, ln2_b):
    p = _prepare_weights(ln_in_g, ln_in_b, w_in, gla_gate_w_fwd, gla_gate_b_fwd, gla_gate_w_bwd,
                         gla_gate_b_bwd, gla_norm_g, na_rpb, w_out, ln1_g, ln1_b, w_up, conv_w, conv_b,
                         w_down, ln2_g, ln2_b)
    return (_trunk(x_prompt, p), _trunk(x_sample, p))
```

```python
import functools

import numpy as np
import jax
import jax.numpy as jnp
from jax import lax
from jax.experimental import pallas as pl
from jax.experimental.pallas import tpu as pltpu

GRID_W = 64
GLA_HEADS, GLA_DK, GLA_DV, GLA_LOWRANK, GLA_TAU, GLA_CHUNK = 4, 64, 128, 16, 16.0, 64
NA_HEADS, NA_DH, NA_WIN_R, NA_WIN_C = 8, 64, 8, 16
LN_EPS, RMS_EPS = 1e-5, 1e-6
DEPTH = 1
DEEPNORM_ALPHA = (2.0 * DEPTH) ** 0.25

LANES = 128
SUBLANES_F32 = 8
SUBLANES_BF16 = 16
VMEM_LIMIT_BYTES = 48 * 1024 * 1024
MLP_VMEM_LIMIT_BYTES = 56 * 1024 * 1024

TOK_TILE = 512
GLA_BLOCK = 1024
SUPER = 2 * GLA_CHUNK
GATE_AHEAD = 2
NA_QROWS = 2
NA_WBLOCKS = 5
NA_PAIRS = 4
FF_BLOCK = 256
HALO = SUBLANES_F32
EDGE = SUBLANES_BF16

_NT = (((1,), (1,)), ((), ()))


def _layer_norm(x, g, b):
    mu = jnp.mean(x, axis=-1, keepdims=True)
    xc = x - mu
    var = jnp.mean(xc * xc, axis=-1, keepdims=True)
    return xc * lax.rsqrt(var + LN_EPS) * g + b


def _cparams(sem, flags=None):
    return pltpu.CompilerParams(dimension_semantics=sem, vmem_limit_bytes=VMEM_LIMIT_BYTES, flags=flags)


SCHED_FLAGS = None


_A_GQ, _A_GK, _A_GG, _A_LR, _A_NQ, _A_NV, _A_END = 0, 256, 512, 1024, 1152, 1664, 2176


def _in_proj_kernel(x_ref, g_ref, b_ref, wa_ref, wt_ref,
                    gq_ref, gk_ref, gg_ref, lr_ref, nq_ref, nv_ref, gvt_ref, nkt_ref):
    xn = _layer_norm(x_ref[0], g_ref[...], b_ref[...]).astype(jnp.bfloat16)

    def proj(lo, hi, out_ref):
        out_ref[0] = jnp.dot(xn, wa_ref[:, lo:hi], preferred_element_type=jnp.float32).astype(out_ref.dtype)

    proj(_A_GQ, _A_GK, gq_ref)
    proj(_A_GK, _A_GG, gk_ref)
    proj(_A_GG, _A_LR, gg_ref)
    proj(_A_LR, _A_NQ, lr_ref)
    proj(_A_NQ, _A_NV, nq_ref)
    proj(_A_NV, _A_END, nv_ref)
    gvt_ref[0] = lax.dot_general(wt_ref[0:512, :], xn, _NT,
                                 preferred_element_type=jnp.float32).astype(gvt_ref.dtype)
    nkt_ref[0] = lax.dot_general(wt_ref[512:1024, :], xn, _NT,
                                 preferred_element_type=jnp.float32).astype(nkt_ref.dtype)


def _in_proj(x, g, b, wa, wt):
    B, T, D = x.shape
    tm = TOK_TILE
    tok = lambda n: pl.BlockSpec((1, tm, n), lambda bi, ti: (bi, ti, 0))
    tr = pl.BlockSpec((1, 512, tm), lambda bi, ti: (bi, 0, ti))
    full = lambda a: pl.BlockSpec(a.shape, lambda bi, ti: (0,) * a.ndim)
    bf = jnp.bfloat16
    out_shape = (
        jax.ShapeDtypeStruct((B, T, 256), bf), jax.ShapeDtypeStruct((B, T, 256), bf),
        jax.ShapeDtypeStruct((B, T, 512), bf), jax.ShapeDtypeStruct((B, T, 128), bf),
        jax.ShapeDtypeStruct((B, T, 512), bf), jax.ShapeDtypeStruct((B, T, 512), bf),
        jax.ShapeDtypeStruct((B, 512, T), bf), jax.ShapeDtypeStruct((B, 512, T), bf),
    )
    return pl.pallas_call(
        _in_proj_kernel,
        grid=(B, T // tm),
        in_specs=[tok(D), full(g), full(b), full(wa), full(wt)],
        out_specs=(tok(256), tok(256), tok(512), tok(128), tok(512), tok(512), tr, tr),
        out_shape=out_shape,
        compiler_params=_cparams(("parallel", "parallel")),
        name="in_proj",
    )(x, g, b, wa, wt)


def _log_sigmoid(z):
    return jnp.minimum(z, 0.0) - jnp.log(1.0 + jnp.exp(-jnp.abs(z)))


def _gla_kernel(*refs, reverse, final):
    if final:
        (q_ref, k_ref, vt_ref, lr_ref, wg_ref, bg_ref, tri_ref, gg_ref, ob_ref, ng_ref,
         out_ref, st_ref) = refs
    else:
        (q_ref, k_ref, vt_ref, lr_ref, wg_ref, bg_ref, tri_ref, out_ref, st_ref) = refs
    C = GLA_CHUNK
    tb = q_ref.shape[1]
    nch, nsc = tb // C, tb // SUPER
    bf = jnp.bfloat16

    @pl.when(pl.program_id(1) == 0)
    def _():
        st_ref[...] = jnp.zeros_like(st_ref)

    tri = tri_ref[...]
    la_parts, bcums = {}, {}

    def gate_logits(sc):
        rows = slice(sc * SUPER, (sc + 1) * SUPER)
        z = jnp.dot(lr_ref[0, rows, :], wg_ref[...], preferred_element_type=jnp.float32) + bg_ref[...]
        la = _log_sigmoid(z) * (1.0 / GLA_TAU)
        la_hi = la.astype(bf)
        la_parts[sc] = (la_hi, (la - la_hi.astype(jnp.float32)).astype(bf))

    def gate_cumsum(sc):
        la_hi, la_lo = la_parts.pop(sc)
        bcums[sc] = (jnp.dot(tri, la_hi, preferred_element_type=jnp.float32)
                     + jnp.dot(tri, la_lo, preferred_element_type=jnp.float32))

    tot_row = 0 if reverse else C - 1

    lane = lax.broadcasted_iota(jnp.int32, (SUPER, LANES), 1)
    row = lax.broadcasted_iota(jnp.int32, (SUPER, LANES), 0)
    head_lane = [lane < GLA_DK, lane >= GLA_DK]
    same_chunk = (row < C) == (lane < C)
    amask = same_chunk & ((lane >= row) if reverse else (lane <= row))
    zc = jnp.zeros((C, LANES), bf)

    qm, a_sc, ut, dec, st_at = {}, {}, {}, {}, {}
    st = [st_ref[0], st_ref[1]]

    def products(sc):
        rows = slice(sc * SUPER, (sc + 1) * SUPER)
        bcum = bcums.pop(sc)
        for cc in range(2):
            dec[2 * sc + cc] = jnp.exp(bcum[cc * C + tot_row:cc * C + tot_row + 1, :])
        q_in = (q_ref[0, rows, :].astype(jnp.float32) * jnp.exp(bcum)).astype(bf)
        k_in32 = k_ref[0, rows, :].astype(jnp.float32) * jnp.exp(-bcum)
        k_in = k_in32.astype(bf)
        k_end = jnp.concatenate([(k_in32[cc * C:(cc + 1) * C] * dec[2 * sc + cc]).astype(bf) for cc in range(2)],
                                axis=0)
        for g in range(2):
            lanes = slice(g * LANES, (g + 1) * LANES)
            kin_sc, kend_sc = k_in[:, lanes], k_end[:, lanes]
            rhs = jnp.concatenate([jnp.concatenate([kend_sc[0:C], zc], axis=0),
                                   jnp.concatenate([zc, kend_sc[C:SUPER]], axis=0)], axis=1)
            ut_all = jnp.dot(vt_ref[0, g * 2 * GLA_DV:(g + 1) * 2 * GLA_DV, rows], rhs,
                             preferred_element_type=jnp.float32)
            for cc in range(2):
                cl = slice(cc * LANES, (cc + 1) * LANES)
                ut[2 * sc + cc, g] = jnp.where(head_lane[0], ut_all[0:GLA_DV, cl], ut_all[GLA_DV:2 * GLA_DV, cl])
            for hh in range(2):
                h = 2 * g + hh
                qp = q_in[:, lanes]
                qm[sc, h] = jnp.where(head_lane[hh], qp, jnp.zeros_like(qp))
                a = lax.dot_general(qm[sc, h], kin_sc, _NT, preferred_element_type=jnp.float32)
                a_sc[sc, h] = jnp.where(amask, a, 0.0).astype(bf)

    def advance_state(sc):
        for c in ((2 * sc + 1, 2 * sc) if reverse else (2 * sc, 2 * sc + 1)):
            for g in range(2):
                st_at[c, g] = st[g].astype(bf)
                st[g] = st[g] * dec[c][:, g * LANES:(g + 1) * LANES] + ut.pop((c, g))

    def outputs(sc):
        rows = slice(sc * SUPER, (sc + 1) * SUPER)
        for h in range(GLA_HEADS):
            g = h // 2
            q2 = qm[sc, h]
            lhs = jnp.concatenate([jnp.concatenate([q2[0:C], zc], axis=1),
                                   jnp.concatenate([zc, q2[C:SUPER]], axis=1)], axis=0)
            lhs = jnp.concatenate([lhs, a_sc[sc, h]], axis=1)
            rhs = jnp.concatenate([st_at[2 * sc, g], st_at[2 * sc + 1, g],
                                   vt_ref[0, h * GLA_DV:(h + 1) * GLA_DV, rows]], axis=1)
            o = lax.dot_general(lhs, rhs, _NT, preferred_element_type=jnp.float32)
            cols = slice(h * GLA_DV, (h + 1) * GLA_DV)
            if final:
                o = o + ob_ref[0, rows, cols]
                o = o * lax.rsqrt(jnp.mean(o * o, axis=-1, keepdims=True) + RMS_EPS) * ng_ref[...]
                gate = gg_ref[0, rows, cols].astype(jnp.float32)
                o = o * (gate * (1.0 / (1.0 + jnp.exp(-gate))))
            out_ref[0, rows, cols] = o.astype(out_ref.dtype)

    order = list(range(nsc - 1, -1, -1) if reverse else range(nsc))
    ahead = nsc if final else GATE_AHEAD
    for sc in order[:ahead]:
        gate_logits(sc)
    for sc in order[:ahead - 1]:
        gate_cumsum(sc)
    for i, sc in enumerate(order):
        if i + ahead < nsc:
            gate_logits(order[i + ahead])
        if i + ahead - 1 < nsc:
            gate_cumsum(order[i + ahead - 1])
        products(sc)
        advance_state(sc)
        if i > 0:
            outputs(order[i - 1])
    outputs(order[-1])
    st_ref[0] = st[0]
    st_ref[1] = st[1]


def _gla(q, k, vt, lr, wg, bg, tri, *, reverse, gg=None, ob=None, ng=None):
    B, T, _ = q.shape
    tb = GLA_BLOCK
    nb = T // tb
    final = gg is not None
    blk = (lambda bi, i: (bi, nb - 1 - i, 0)) if reverse else (lambda bi, i: (bi, i, 0))
    blk_t = (lambda bi, i: (bi, 0, nb - 1 - i)) if reverse else (lambda bi, i: (bi, 0, i))
    tok = lambda n: pl.BlockSpec((1, tb, n), blk)
    full = lambda a: pl.BlockSpec(a.shape, lambda bi, i: (0,) * a.ndim)
    in_specs = [tok(256), tok(256), pl.BlockSpec((1, 512, tb), blk_t), tok(128), full(wg), full(bg), full(tri)]
    args = [q, k, vt, lr, wg, bg, tri]
    if final:
        in_specs += [tok(512), tok(512), full(ng)]
        args += [gg, ob, ng]
    return pl.pallas_call(
        functools.partial(_gla_kernel, reverse=reverse, final=final),
        grid=(B, nb),
        in_specs=in_specs,
        out_specs=tok(512),
        out_shape=jax.ShapeDtypeStruct((B, T, 512), jnp.bfloat16 if final else jnp.float32),
        scratch_shapes=[pltpu.VMEM((2, GLA_DV, 2 * GLA_DK), jnp.float32)],
        compiler_params=_cparams(("parallel", "arbitrary"), SCHED_FLAGS),
        name="gla_fwd" if final else "gla_bwd",
    )(*args)


def _cumsum_operator(reverse):
    i = np.arange(SUPER)
    same = (i[:, None] // GLA_CHUNK) == (i[None, :] // GLA_CHUNK)
    tri = (i[None, :] >= i[:, None]) if reverse else (i[None, :] <= i[:, None])
    return jnp.asarray(same & tri, jnp.bfloat16)


def _na_window_block(p, npairs):
    return jnp.clip(p - 2, 0, npairs - NA_WBLOCKS)


def _na_pair_type(p, npairs):
    return jnp.where(p < 2, p, jnp.where(p >= npairs - 2, p - (npairs - 5), 2))


def _na_bias_table(rpb):
    rows = 64
    npairs = rows // 2
    wk = NA_WBLOCKS * NA_QROWS
    nh = rpb.shape[0]
    ring = jnp.concatenate([rpb[..., NA_WIN_C - 1:], jnp.zeros(rpb.shape[:2] + (LANES - (2 * NA_WIN_C - 1),), rpb.dtype),
                            rpb[..., :NA_WIN_C - 1]], axis=-1)
    toep = jnp.tile(ring, (1, 1, GRID_W))[..., :GRID_W * (LANES - 1)]
    toep = toep.reshape(nh, rpb.shape[1], GRID_W, LANES - 1)[..., :GRID_W]
    c = np.arange(GRID_W)
    qcs = np.clip(c - NA_WIN_C // 2, 0, GRID_W - NA_WIN_C)
    col_ok = (c[None, :] >= qcs[:, None]) & (c[None, :] < qcs[:, None] + NA_WIN_C)
    toep = jnp.where(col_ok[None, None], toep, -jnp.inf)
    masked = jnp.full((nh, GRID_W, GRID_W), -jnp.inf, rpb.dtype)
    types = []
    for p in (0, 1, 2, npairs - 2, npairs - 1):
        w0 = 2 * int(np.clip(p - 2, 0, npairs - NA_WBLOCKS))
        qrows = []
        for a in range(NA_QROWS):
            r = 2 * p + a
            rs = int(np.clip(r - NA_WIN_R // 2, 0, rows - NA_WIN_R))
            blocks = [toep[:, w0 + i - r + NA_WIN_R - 1] if rs <= w0 + i < rs + NA_WIN_R else masked
                      for i in range(wk)]
            qrows.append(jnp.concatenate(blocks, axis=-1))
        types.append(jnp.concatenate(qrows, axis=-2))
    return jnp.stack(types, axis=0).astype(jnp.float32)


def _natten_kernel(*refs, npairs):
    q_ref, bias_ref, out_ref = refs[0], refs[1], refs[-1]
    nq = NA_QROWS * GRID_W
    nk = nq * NA_WBLOCKS
    lane = lax.broadcasted_iota(jnp.int32, (nq, LANES), 1)
    head_lane = [lane < NA_DH, lane >= NA_DH]
    ones = jnp.ones((nk, LANES), jnp.bfloat16)
    pair_type = [_na_pair_type(NA_PAIRS * pl.program_id(1) + j, npairs) for j in range(NA_PAIRS)]

    def pair_refs(j):
        return refs[2 + 2 * j:4 + 2 * j]

    def qk(j, h):
        kt_ref, _ = pair_refs(j)
        g, hh = divmod(h, 2)
        lanes = slice(g * LANES, (g + 1) * LANES)
        qp = q_ref[0, j * nq:(j + 1) * nq, lanes]
        qm = jnp.where(head_lane[hh], qp, jnp.zeros_like(qp))
        return jnp.dot(qm, kt_ref[lanes, :], preferred_element_type=jnp.float32) + bias_ref[pair_type[j], h]

    def softmax_num(s):
        m = jnp.max(s, axis=-1, keepdims=True)
        return jnp.exp(s - m).astype(jnp.bfloat16)

    def pv(j, h, p):
        _, v_ref = pair_refs(j)
        lanes = slice((h // 2) * LANES, (h // 2 + 1) * LANES)
        v1 = jnp.concatenate([v_ref[:, lanes], ones], axis=1)
        o = jnp.dot(p, v1, preferred_element_type=jnp.float32)
        return o[:, 0:LANES] * (1.0 / o[:, LANES:2 * LANES])

    scores = {(j, h): qk(j, h) for j in range(NA_PAIRS) for h in range(NA_HEADS)}
    for j in range(NA_PAIRS):
        for g in range(NA_HEADS // 2):
            o0 = pv(j, 2 * g, softmax_num(scores.pop((j, 2 * g))))
            o1 = pv(j, 2 * g + 1, softmax_num(scores.pop((j, 2 * g + 1))))
            out_ref[0, j * nq:(j + 1) * nq, g * LANES:(g + 1) * LANES] = \
                jnp.where(head_lane[0], o0, o1).astype(out_ref.dtype)


def _natten(q, kt, v, bias):
    B, T, _ = q.shape
    nq = NA_QROWS * GRID_W
    npairs = T // nq
    q_spec = pl.BlockSpec((1, NA_PAIRS * nq, 512), lambda bi, s: (bi, s, 0))
    bias_spec = pl.BlockSpec(bias.shape, lambda bi, s: (0, 0, 0, 0), pipeline_mode=pl.Buffered(1))
    in_specs, args = [q_spec, bias_spec], [q, bias]
    nk = nq * NA_WBLOCKS
    for j in range(NA_PAIRS):
        pair = functools.partial(lambda s, j: NA_PAIRS * s + j, j=j)
        in_specs += [
            pl.BlockSpec((None, pl.Element(512), pl.Element(nk)), functools.partial(
                lambda bi, s, pair: (bi, 0, _na_window_block(pair(s), npairs) * nq), pair=pair)),
            pl.BlockSpec((None, pl.Element(nk), pl.Element(512)), functools.partial(
                lambda bi, s, pair: (bi, _na_window_block(pair(s), npairs) * nq, 0), pair=pair))]
        args += [kt, v]
    return pl.pallas_call(
        functools.partial(_natten_kernel, npairs=npairs),
        grid=(B, npairs // NA_PAIRS),
        in_specs=in_specs,
        out_specs=q_spec,
        out_shape=jax.ShapeDtypeStruct((B, T, 512), jnp.bfloat16),
        compiler_params=_cparams(("parallel", "arbitrary"), SCHED_FLAGS),
        name="natten",
    )(*args)


def _tail_kernel(x_ref, xt_ref, xb_ref, og_ref, ogt_ref, ogb_ref, on_ref, ont_ref, onb_ref, wo_ref,
                 gi_ref, bi_ref, g1_ref, b1_ref, wup_ref, cw_ref, cb_ref, wd_ref, g2_ref, b2_ref, out_ref,
                 x1_ref, lhs_ref, act_ref):
    t = pl.program_id(1)
    tm = x_ref.shape[1]
    d_ff = wd_ref.shape[0]
    half = tm // 2
    span = half + 2 * EDGE
    rows = half + 2 * HALO
    bf = jnp.bfloat16

    def with_edges(s, ref, top_ref, bot_ref):
        if s == 0:
            return jnp.concatenate([top_ref[0], ref[0, 0:half + EDGE]], axis=0)
        return jnp.concatenate([ref[0, half - EDGE:tm], bot_ref[0]], axis=0)

    def first_stage(s):
        xn = _layer_norm(with_edges(s, x_ref, xt_ref, xb_ref), gi_ref[...], bi_ref[...])
        mix = (jnp.dot(with_edges(s, og_ref, ogt_ref, ogb_ref), wo_ref[0:512, :],
                       preferred_element_type=jnp.float32)
               + jnp.dot(with_edges(s, on_ref, ont_ref, onb_ref), wo_ref[512:1024, :],
                         preferred_element_type=jnp.float32))
        x1 = _layer_norm(DEEPNORM_ALPHA * xn + mix, g1_ref[...], b1_ref[...])
        x1_ref[s] = x1[EDGE:EDGE + half]
        top, bot = x1[EDGE - HALO:EDGE], x1[EDGE + half:EDGE + half + HALO]
        if s == 0:
            top = jnp.where(t > 0, top, 0.0)
        else:
            bot = jnp.where(t < pl.num_programs(1) - 1, bot, 0.0)
        lhs_ref[s] = jnp.concatenate([top, x1[EDGE:EDGE + half], bot], axis=0).astype(bf)

    def conv(h, cols):
        prev = pltpu.roll(h, 1, 0)[HALO:HALO + half]
        nxt = pltpu.roll(h, rows - 1, 0)[HALO:HALO + half]
        return (prev * cw_ref[0:1, cols] + h[HALO:HALO + half] * cw_ref[1:2, cols] + nxt * cw_ref[2:3, cols]
                + cb_ref[:, cols])

    def mlp_matmuls(s):
        lhs = lhs_ref[s]
        for f in range(d_ff // FF_BLOCK):
            cols_a = slice(f * FF_BLOCK, (f + 1) * FF_BLOCK)
            cols_g = slice(d_ff + f * FF_BLOCK, d_ff + (f + 1) * FF_BLOCK)
            a = conv(jnp.dot(lhs, wup_ref[:, cols_a], preferred_element_type=jnp.float32), cols_a)
            gt = conv(jnp.dot(lhs, wup_ref[:, cols_g], preferred_element_type=jnp.float32), cols_g)
            act_ref[s, :, cols_a] = (gt * (1.0 / (1.0 + jnp.exp(-gt))) * a).astype(bf)
        return jnp.dot(act_ref[s], wd_ref[...], preferred_element_type=jnp.float32)

    def final_norm(s, f_out):
        out_ref[0, s * half:(s + 1) * half, :] = _layer_norm(DEEPNORM_ALPHA * x1_ref[s] + f_out,
                                                             g2_ref[...], b2_ref[...])

    first_stage(0)
    first_stage(1)
    f_out0 = mlp_matmuls(0)
    f_out1 = mlp_matmuls(1)
    final_norm(0, f_out0)
    final_norm(1, f_out1)


def _tail(x, og, on, w_out, gi, bi, g1, b1, w_up, conv_w, conv_b, w_down, g2, b2):
    B, T, D = x.shape
    tm = TOK_TILE
    d_ff = w_down.shape[0]
    eb = tm // EDGE
    last_eb = T // EDGE - 1
    tok = lambda n: pl.BlockSpec((1, tm, n), lambda b_, t: (b_, t, 0))
    top = lambda n: pl.BlockSpec((1, EDGE, n), lambda b_, t: (b_, jnp.maximum(t * eb - 1, 0), 0))
    bot = lambda n: pl.BlockSpec((1, EDGE, n), lambda b_, t: (b_, jnp.minimum((t + 1) * eb, last_eb), 0))
    full = lambda a: pl.BlockSpec(a.shape, lambda b_, t: (0,) * a.ndim)
    resident = lambda a: pl.BlockSpec(a.shape, lambda b_, t: (0,) * a.ndim, pipeline_mode=pl.Buffered(1))
    return pl.pallas_call(
        _tail_kernel,
        grid=(B, T // tm),
        in_specs=[tok(D), top(D), bot(D), tok(512), top(512), bot(512), tok(512), top(512), bot(512),
                  resident(w_out), full(gi), full(bi), full(g1), full(b1),
                  resident(w_up), full(conv_w), full(conv_b), resident(w_down), full(g2), full(b2)],
        out_specs=tok(D),
        out_shape=jax.ShapeDtypeStruct((B, T, D), jnp.float32),
        scratch_shapes=[pltpu.VMEM((2, tm // 2, D), jnp.float32),
                        pltpu.VMEM((2, tm // 2 + 2 * HALO, D), jnp.bfloat16),
                        pltpu.VMEM((2, tm // 2, d_ff), jnp.bfloat16)],
        compiler_params=pltpu.CompilerParams(dimension_semantics=("parallel", "parallel"),
                                             vmem_limit_bytes=MLP_VMEM_LIMIT_BYTES),
        name="tail",
    )(x, x, x, og, og, og, on, on, on, w_out, gi, bi, g1, b1, w_up, conv_w, conv_b, w_down, g2, b2)


def _prepare_weights(ln_in_g, ln_in_b, w_in, gw_f, gb_f, gw_b, gb_b, gla_norm_g, na_rpb, w_out,
                     ln1_g, ln1_b, w_up, conv_w, conv_b, w_down, ln2_g, ln2_b):
    bf = jnp.bfloat16
    row = lambda a: a.reshape(1, -1).astype(jnp.float32)
    w = w_in[0]
    sizes = (256, 256, 512, 512, 32, 512, 512, 512)
    offs = np.concatenate([[0], np.cumsum(sizes)])
    gq, gk, gv, gg, glr, nq, nk, nv = [w[:, offs[i]:offs[i + 1]] for i in range(8)]
    lr_pad = jnp.zeros((w.shape[0], LANES - 2 * GLA_LOWRANK), w.dtype)
    wa = jnp.concatenate([gq * (GLA_DK ** -0.5), gk, gg, glr, lr_pad, nq * (NA_DH ** -0.5), nv], axis=1).astype(bf)
    wt = jnp.concatenate([gv, nk], axis=1).T.astype(bf)
    zpad = jnp.zeros((LANES - 2 * GLA_LOWRANK, gw_f.shape[-1]), jnp.float32)
    zlr = jnp.zeros((GLA_LOWRANK, gw_f.shape[-1]), jnp.float32)
    wg_f = jnp.concatenate([gw_f[0], zlr, zpad], axis=0).astype(bf)
    wg_b = jnp.concatenate([zlr, gw_b[0], zpad], axis=0).astype(bf)
    bias = _na_bias_table(na_rpb[0])
    return dict(
        ln_in=(row(ln_in_g), row(ln_in_b)), wa=wa, wt=wt,
        wg_f=wg_f, bg_f=row(gb_f[0]), wg_b=wg_b, bg_b=row(gb_b[0]), ng=row(gla_norm_g[0]),
        tri_f=_cumsum_operator(False), tri_b=_cumsum_operator(True),
        bias=bias, w_out=w_out[0].astype(bf), ln1=(row(ln1_g[0]), row(ln1_b[0])),
        w_up=w_up[0].astype(bf), conv_w=conv_w[0].astype(jnp.float32), conv_b=row(conv_b[0]),
        w_down=w_down[0].astype(bf), ln2=(row(ln2_g[0]), row(ln2_b[0])),
    )


def _trunk(x, p):
    gq, gk, gg, lr, nq, nv, gvt, nkt = _in_proj(x, *p["ln_in"], p["wa"], p["wt"])
    o_bwd = _gla(gq, gk, gvt, lr, p["wg_b"], p["bg_b"], p["tri_b"], reverse=True)
    o_gla = _gla(gq, gk, gvt, lr, p["wg_f"], p["bg_f"], p["tri_f"], reverse=False, gg=gg, ob=o_bwd, ng=p["ng"])
    o_na = _natten(nq, nkt, nv, p["bias"])
    return _tail(x, o_gla, o_na, p["w_out"], *p["ln_in"], *p["ln1"],
                 p["w_up"], p["conv_w"], p["conv_b"], p["w_down"], *p["ln2"])


def kernel(x_prompt, x_sample, ln_in_g, ln_in_b, w_in, gla_gate_w_fwd, gla_gate_b_fwd, gla_gate_w_bwd,
           gla_gate_b_bwd, gla_norm_g, na_rpb, w_out, ln1_g, ln1_b, w_up, conv_w, conv_b, w_down, ln2_g, ln2_b):
    p = _prepare_weights(ln_in_g, ln_in_b, w_in, gla_gate_w_fwd, gla_gate_b_fwd, gla_gate_w_bwd,
                         gla_gate_b_bwd, gla_norm_g, na_rpb, w_out, ln1_g, ln1_b, w_up, conv_w, conv_b,
                         w_down, ln2_g, ln2_b)
    return (_trunk(x_prompt, p), _trunk(x_sample, p))
```

```python
import functools

import numpy as np
import jax
import jax.numpy as jnp
from jax import lax
from jax.experimental import pallas as pl
from jax.experimental.pallas import tpu as pltpu

GRID_W = 64
GLA_HEADS, GLA_DK, GLA_DV, GLA_LOWRANK, GLA_TAU, GLA_CHUNK = 4, 64, 128, 16, 16.0, 64
NA_HEADS, NA_DH, NA_WIN_R, NA_WIN_C = 8, 64, 8, 16
LN_EPS, RMS_EPS = 1e-5, 1e-6
DEPTH = 1
DEEPNORM_ALPHA = (2.0 * DEPTH) ** 0.25

LANES = 128
SUBLANES_F32 = 8
SUBLANES_BF16 = 16
VMEM_LIMIT_BYTES = 48 * 1024 * 1024
MLP_VMEM_LIMIT_BYTES = 56 * 1024 * 1024

TOK_TILE = 512
GLA_BLOCK = 1024
SUPER = 2 * GLA_CHUNK
GATE_AHEAD = 2
NA_QROWS = 2
NA_WBLOCKS = 5
NA_PAIRS = 4
FF_BLOCK = 256
HALO = SUBLANES_F32
EDGE = SUBLANES_BF16

_NT = (((1,), (1,)), ((), ()))


def _layer_norm(x, g, b):
    mu = jnp.mean(x, axis=-1, keepdims=True)
    xc = x - mu
    var = jnp.mean(xc * xc, axis=-1, keepdims=True)
    return xc * lax.rsqrt(var + LN_EPS) * g + b


def _cparams(sem):
    return pltpu.CompilerParams(dimension_semantics=sem, vmem_limit_bytes=VMEM_LIMIT_BYTES)


_A_GQ, _A_GK, _A_GG, _A_LR, _A_NK, _A_END = 0, 256, 512, 1024, 1152, 1664


def _in_proj_kernel(x_ref, g_ref, b_ref, wa_ref, wt_ref,
                    gq_ref, gk_ref, gs_ref, lr_ref, nk_ref, gvt_ref, nqt_ref, nvt_ref):
    tm = x_ref.shape[1]
    half = tm // 2
    xn = [_layer_norm(x_ref[0, s * half:(s + 1) * half, :], g_ref[...], b_ref[...]).astype(jnp.bfloat16)
          for s in range(2)]
    for s in range(2):
        rows = slice(s * half, (s + 1) * half)
        for lo, hi, out_ref in ((_A_GQ, _A_GK, gq_ref), (_A_GK, _A_GG, gk_ref), (_A_GG, _A_LR, gs_ref),
                                (_A_LR, _A_NK, lr_ref), (_A_NK, _A_END, nk_ref)):
            y = jnp.dot(xn[s], wa_ref[:, lo:hi], preferred_element_type=jnp.float32)
            if out_ref is gs_ref:
                y = y * (1.0 / (1.0 + jnp.exp(-y)))
            out_ref[0, rows, :] = y.astype(out_ref.dtype)
        for i, out_ref in enumerate((gvt_ref, nqt_ref, nvt_ref)):
            out_ref[0, :, rows] = lax.dot_general(wt_ref[i * 512:(i + 1) * 512, :], xn[s], _NT,
                                                  preferred_element_type=jnp.float32).astype(out_ref.dtype)


def _in_proj(x, g, b, wa, wt):
    B, T, D = x.shape
    tm = TOK_TILE
    tok = lambda n: pl.BlockSpec((1, tm, n), lambda bi, ti: (bi, ti, 0))
    tr = pl.BlockSpec((1, 512, tm), lambda bi, ti: (bi, 0, ti))
    full = lambda a: pl.BlockSpec(a.shape, lambda bi, ti: (0,) * a.ndim)
    bf = jnp.bfloat16
    out_shape = (
        jax.ShapeDtypeStruct((B, T, 256), bf), jax.ShapeDtypeStruct((B, T, 256), bf),
        jax.ShapeDtypeStruct((B, T, 512), bf), jax.ShapeDtypeStruct((B, T, 128), bf),
        jax.ShapeDtypeStruct((B, T, 512), bf),
        jax.ShapeDtypeStruct((B, 512, T), bf), jax.ShapeDtypeStruct((B, 512, T), bf),
        jax.ShapeDtypeStruct((B, 512, T), bf),
    )
    return pl.pallas_call(
        _in_proj_kernel,
        grid=(B, T // tm),
        in_specs=[tok(D), full(g), full(b), full(wa), full(wt)],
        out_specs=(tok(256), tok(256), tok(512), tok(128), tok(512), tr, tr, tr),
        out_shape=out_shape,
        compiler_params=_cparams(("parallel", "parallel")),
        name="in_proj",
    )(x, g, b, wa, wt)


def _log_sigmoid(z):
    return jnp.minimum(z, 0.0) - jnp.log(1.0 + jnp.exp(-jnp.abs(z)))


def _gla_kernel(*refs, reverse, final):
    if final:
        (q_ref, k_ref, vt_ref, lr_ref, wg_ref, bg_ref, tri_ref, gs_ref, ob_ref, ng_ref,
         out_ref, st_ref) = refs
    else:
        (q_ref, k_ref, vt_ref, lr_ref, wg_ref, bg_ref, tri_ref, out_ref, st_ref) = refs
    C = GLA_CHUNK
    tb = q_ref.shape[1]
    nsc = tb // SUPER
    bf = jnp.bfloat16

    @pl.when(pl.program_id(1) == 0)
    def _():
        st_ref[...] = jnp.zeros_like(st_ref)

    tri = tri_ref[...]
    la_parts, bcums = {}, {}

    def gate_logits(sc):
        rows = slice(sc * SUPER, (sc + 1) * SUPER)
        z = jnp.dot(lr_ref[0, rows, :], wg_ref[...], preferred_element_type=jnp.float32) + bg_ref[...]
        la = _log_sigmoid(z) * (1.0 / GLA_TAU)
        la_hi = la.astype(bf)
        la_parts[sc] = (la_hi, (la - la_hi.astype(jnp.float32)).astype(bf))

    def gate_cumsum(sc):
        la_hi, la_lo = la_parts.pop(sc)
        bcums[sc] = (jnp.dot(tri, la_hi, preferred_element_type=jnp.float32)
                     + jnp.dot(tri, la_lo, preferred_element_type=jnp.float32))

    tot_row = 0 if reverse else C - 1

    lane = lax.broadcasted_iota(jnp.int32, (SUPER, LANES), 1)
    row = lax.broadcasted_iota(jnp.int32, (SUPER, LANES), 0)
    head_lane = [lane < GLA_DK, lane >= GLA_DK]
    same_chunk = (row < C) == (lane < C)
    amask = same_chunk & ((lane >= row) if reverse else (lane <= row))
    zc = jnp.zeros((C, LANES), bf)

    qm, a_sc, ut, dec, st_at = {}, {}, {}, {}, {}
    st = [st_ref[0], st_ref[1]]

    def products(sc):
        rows = slice(sc * SUPER, (sc + 1) * SUPER)
        bcum = bcums.pop(sc)
        for cc in range(2):
            dec[2 * sc + cc] = jnp.exp(bcum[cc * C + tot_row:cc * C + tot_row + 1, :])
        q_in = (q_ref[0, rows, :].astype(jnp.float32) * jnp.exp(bcum)).astype(bf)
        k_in32 = k_ref[0, rows, :].astype(jnp.float32) * jnp.exp(-bcum)
        k_in = k_in32.astype(bf)
        k_end = jnp.concatenate([(k_in32[cc * C:(cc + 1) * C] * dec[2 * sc + cc]).astype(bf) for cc in range(2)],
                                axis=0)
        for g in range(2):
            lanes = slice(g * LANES, (g + 1) * LANES)
            kin_sc, kend_sc = k_in[:, lanes], k_end[:, lanes]
            rhs = jnp.concatenate([jnp.concatenate([kend_sc[0:C], zc], axis=0),
                                   jnp.concatenate([zc, kend_sc[C:SUPER]], axis=0)], axis=1)
            ut_all = jnp.dot(vt_ref[0, g * 2 * GLA_DV:(g + 1) * 2 * GLA_DV, rows], rhs,
                             preferred_element_type=jnp.float32)
            for cc in range(2):
                cl = slice(cc * LANES, (cc + 1) * LANES)
                ut[2 * sc + cc, g] = jnp.where(head_lane[0], ut_all[0:GLA_DV, cl], ut_all[GLA_DV:2 * GLA_DV, cl])
            qp = q_in[:, lanes]
            for hh in range(2):
                qm[sc, 2 * g + hh] = jnp.where(head_lane[hh], qp, jnp.zeros_like(qp))
            a = lax.dot_general(jnp.concatenate([qm[sc, 2 * g], qm[sc, 2 * g + 1]], axis=0), kin_sc, _NT,
                                preferred_element_type=jnp.float32)
            for hh in range(2):
                a_sc[sc, 2 * g + hh] = jnp.where(amask, a[hh * SUPER:(hh + 1) * SUPER], 0.0).astype(bf)

    def advance_state(sc):
        for c in ((2 * sc + 1, 2 * sc) if reverse else (2 * sc, 2 * sc + 1)):
            for g in range(2):
                st_at[c, g] = st[g].astype(bf)
                st[g] = st[g] * dec[c][:, g * LANES:(g + 1) * LANES] + ut.pop((c, g))

    def outputs(sc):
        rows = slice(sc * SUPER, (sc + 1) * SUPER)
        for h in range(GLA_HEADS):
            g = h // 2
            q2 = qm[sc, h]
            lhs = jnp.concatenate([jnp.concatenate([q2[0:C], zc], axis=1),
                                   jnp.concatenate([zc, q2[C:SUPER]], axis=1)], axis=0)
            lhs = jnp.concatenate([lhs, a_sc[sc, h]], axis=1)
            rhs = jnp.concatenate([st_at[2 * sc, g], st_at[2 * sc + 1, g],
                                   vt_ref[0, h * GLA_DV:(h + 1) * GLA_DV, rows]], axis=1)
            o = lax.dot_general(lhs, rhs, _NT, preferred_element_type=jnp.float32)
            cols = slice(h * GLA_DV, (h + 1) * GLA_DV)
            if final:
                o = o + ob_ref[0, rows, cols]
                o = o * lax.rsqrt(jnp.mean(o * o, axis=-1, keepdims=True) + RMS_EPS) * ng_ref[...]
                o = o * gs_ref[0, rows, cols].astype(jnp.float32)
            out_ref[0, rows, cols] = o.astype(out_ref.dtype)

    order = list(range(nsc - 1, -1, -1) if reverse else range(nsc))
    ahead = nsc if final else GATE_AHEAD
    for sc in order[:ahead]:
        gate_logits(sc)
    for sc in order[:ahead - 1]:
        gate_cumsum(sc)
    for i, sc in enumerate(order):
        if i + ahead < nsc:
            gate_logits(order[i + ahead])
        if i + ahead - 1 < nsc:
            gate_cumsum(order[i + ahead - 1])
        products(sc)
        advance_state(sc)
        if i > 0:
            outputs(order[i - 1])
    outputs(order[-1])
    st_ref[0] = st[0]
    st_ref[1] = st[1]


def _gla(q, k, vt, lr, wg, bg, tri, *, reverse, gs=None, ob=None, ng=None):
    B, T, _ = q.shape
    tb = GLA_BLOCK
    nb = T // tb
    final = gs is not None
    blk = (lambda bi, i: (bi, nb - 1 - i, 0)) if reverse else (lambda bi, i: (bi, i, 0))
    blk_t = (lambda bi, i: (bi, 0, nb - 1 - i)) if reverse else (lambda bi, i: (bi, 0, i))
    tok = lambda n: pl.BlockSpec((1, tb, n), blk)
    full = lambda a: pl.BlockSpec(a.shape, lambda bi, i: (0,) * a.ndim)
    in_specs = [tok(256), tok(256), pl.BlockSpec((1, 512, tb), blk_t), tok(128), full(wg), full(bg), full(tri)]
    args = [q, k, vt, lr, wg, bg, tri]
    if final:
        in_specs += [tok(512), tok(512), full(ng)]
        args += [gs, ob, ng]
    return pl.pallas_call(
        functools.partial(_gla_kernel, reverse=reverse, final=final),
        grid=(B, nb),
        in_specs=in_specs,
        out_specs=tok(512),
        out_shape=jax.ShapeDtypeStruct((B, T, 512), jnp.bfloat16 if final else jnp.float32),
        scratch_shapes=[pltpu.VMEM((2, GLA_DV, 2 * GLA_DK), jnp.float32)],
        compiler_params=_cparams(("parallel", "arbitrary")),
        name="gla_fwd" if final else "gla_bwd",
    )(*args)


def _cumsum_operator(reverse):
    i = np.arange(SUPER)
    same = (i[:, None] // GLA_CHUNK) == (i[None, :] // GLA_CHUNK)
    tri = (i[None, :] >= i[:, None]) if reverse else (i[None, :] <= i[:, None])
    return jnp.asarray(same & tri, jnp.bfloat16)


def _na_window_block(p, npairs):
    return jnp.clip(p - 2, 0, npairs - NA_WBLOCKS)


def _na_pair_type(p, npairs):
    return jnp.where(p < 2, p, jnp.where(p >= npairs - 2, p - (npairs - 5), 2))


def _na_bias_table(rpb):
    rows = 64
    npairs = rows // 2
    wk = NA_WBLOCKS * NA_QROWS
    nh = rpb.shape[0]
    ring = jnp.concatenate([rpb[..., NA_WIN_C - 1:], jnp.zeros(rpb.shape[:2] + (LANES - (2 * NA_WIN_C - 1),), rpb.dtype),
                            rpb[..., :NA_WIN_C - 1]], axis=-1)
    toep = jnp.tile(ring, (1, 1, GRID_W))[..., :GRID_W * (LANES - 1)]
    toep = toep.reshape(nh, rpb.shape[1], GRID_W, LANES - 1)[..., :GRID_W]
    c = np.arange(GRID_W)
    qcs = np.clip(c - NA_WIN_C // 2, 0, GRID_W - NA_WIN_C)
    col_ok = (c[None, :] >= qcs[:, None]) & (c[None, :] < qcs[:, None] + NA_WIN_C)
    toep = jnp.where(col_ok[None, None], toep, -jnp.inf)
    masked = jnp.full((nh, GRID_W, GRID_W), -jnp.inf, rpb.dtype)
    types = []
    for p in (0, 1, 2, npairs - 2, npairs - 1):
        w0 = 2 * int(np.clip(p - 2, 0, npairs - NA_WBLOCKS))
        qrows = []
        for a in range(NA_QROWS):
            r = 2 * p + a
            rs = int(np.clip(r - NA_WIN_R // 2, 0, rows - NA_WIN_R))
            blocks = [toep[:, w0 + i - r + NA_WIN_R - 1] if rs <= w0 + i < rs + NA_WIN_R else masked
                      for i in range(wk)]
            qrows.append(jnp.concatenate(blocks, axis=-1))
        types.append(jnp.concatenate(qrows, axis=-2))
    return jnp.stack(types, axis=0).astype(jnp.float32)


def _natten_kernel(*refs, npairs):
    qt_ref, bias_ref, out_ref = refs[0], refs[1], refs[-1]
    nq = NA_QROWS * GRID_W
    nk = nq * NA_WBLOCKS
    first_head = lax.broadcasted_iota(jnp.int32, (LANES, nq), 0) < NA_DH
    ones = jnp.ones((LANES, nk), jnp.bfloat16)
    pair_type = [_na_pair_type(NA_PAIRS * pl.program_id(1) + j, npairs) for j in range(NA_PAIRS)]

    def pair_refs(j):
        return refs[2 + 2 * j:4 + 2 * j]

    def softmax_num(j, g):
        k_ref, _ = pair_refs(j)
        group = slice(g * LANES, (g + 1) * LANES)
        qt = qt_ref[0, group, j * nq:(j + 1) * nq]
        zero = jnp.zeros_like(qt)
        qm = jnp.concatenate([jnp.where(first_head, qt, zero), jnp.where(first_head, zero, qt)], axis=1)
        s = jnp.dot(k_ref[:, group], qm, preferred_element_type=jnp.float32) + bias_ref[pair_type[j], g]
        m = jnp.max(s, axis=0, keepdims=True)
        return jnp.exp(s - m).astype(jnp.bfloat16)

    def pv(j, g, p):
        _, vt_ref = pair_refs(j)
        group = slice(g * LANES, (g + 1) * LANES)
        v1t = jnp.concatenate([vt_ref[group, :], ones], axis=0)
        o = jnp.dot(v1t, p, preferred_element_type=jnp.float32)
        num = jnp.where(first_head, o[0:LANES, 0:nq], o[0:LANES, nq:2 * nq])
        den = jnp.where(first_head, o[LANES:2 * LANES, 0:nq], o[LANES:2 * LANES, nq:2 * nq])
        return (num * (1.0 / den)).T

    probs = {(j, g): softmax_num(j, g) for j in range(NA_PAIRS) for g in range(NA_HEADS // 2)}
    for j in range(NA_PAIRS):
        for g in range(NA_HEADS // 2):
            out_ref[0, j * nq:(j + 1) * nq, g * LANES:(g + 1) * LANES] = \
                pv(j, g, probs.pop((j, g))).astype(out_ref.dtype)


def _natten(qt, k, vt, bias):
    B, _, T = qt.shape
    nq = NA_QROWS * GRID_W
    npairs = T // nq
    qt_spec = pl.BlockSpec((1, 512, NA_PAIRS * nq), lambda bi, s: (bi, 0, s))
    out_spec = pl.BlockSpec((1, NA_PAIRS * nq, 512), lambda bi, s: (bi, s, 0))
    bias_spec = pl.BlockSpec(bias.shape, lambda bi, s: (0, 0, 0, 0), pipeline_mode=pl.Buffered(1))
    in_specs, args = [qt_spec, bias_spec], [qt, bias]
    nk = nq * NA_WBLOCKS
    for j in range(NA_PAIRS):
        pair = functools.partial(lambda s, j: NA_PAIRS * s + j, j=j)
        in_specs += [
            pl.BlockSpec((None, pl.Element(nk), pl.Element(512)), functools.partial(
                lambda bi, s, pair: (bi, _na_window_block(pair(s), npairs) * nq, 0), pair=pair)),
            pl.BlockSpec((None, pl.Element(512), pl.Element(nk)), functools.partial(
                lambda bi, s, pair: (bi, 0, _na_window_block(pair(s), npairs) * nq), pair=pair))]
        args += [k, vt]
    return pl.pallas_call(
        functools.partial(_natten_kernel, npairs=npairs),
        grid=(B, npairs // NA_PAIRS),
        in_specs=in_specs,
        out_specs=out_spec,
        out_shape=jax.ShapeDtypeStruct((B, T, 512), jnp.bfloat16),
        compiler_params=_cparams(("parallel", "arbitrary")),
        name="natten",
    )(*args)


def _tail_kernel(x_ref, xt_ref, xb_ref, og_ref, ogt_ref, ogb_ref, on_ref, ont_ref, onb_ref, wo_ref,
                 gi_ref, bi_ref, g1_ref, b1_ref, wup_ref, cw_ref, cb_ref, wd_ref, g2_ref, b2_ref, out_ref,
                 x1_ref, lhs_ref, act_ref):
    t = pl.program_id(1)
    tm = x_ref.shape[1]
    d_ff = wd_ref.shape[0]
    half = tm // 2
    rows = half + 2 * HALO
    bf = jnp.bfloat16

    def with_edges(s, ref, top_ref, bot_ref):
        if s == 0:
            return jnp.concatenate([top_ref[0], ref[0, 0:half + EDGE]], axis=0)
        return jnp.concatenate([ref[0, half - EDGE:tm], bot_ref[0]], axis=0)

    def first_stage(s):
        xn = _layer_norm(with_edges(s, x_ref, xt_ref, xb_ref), gi_ref[...], bi_ref[...])
        mix = (jnp.dot(with_edges(s, og_ref, ogt_ref, ogb_ref), wo_ref[0:512, :],
                       preferred_element_type=jnp.float32)
               + jnp.dot(with_edges(s, on_ref, ont_ref, onb_ref), wo_ref[512:1024, :],
                         preferred_element_type=jnp.float32))
        x1 = _layer_norm(DEEPNORM_ALPHA * xn + mix, g1_ref[...], b1_ref[...])
        x1_ref[s] = x1[EDGE:EDGE + half]
        top, bot = x1[EDGE - HALO:EDGE], x1[EDGE + half:EDGE + half + HALO]
        if s == 0:
            top = jnp.where(t > 0, top, 0.0)
        else:
            bot = jnp.where(t < pl.num_programs(1) - 1, bot, 0.0)
        lhs_ref[s] = jnp.concatenate([top, x1[EDGE:EDGE + half], bot], axis=0).astype(bf)

    def conv(h, cols):
        prev = pltpu.roll(h, 1, 0)[HALO:HALO + half]
        nxt = pltpu.roll(h, rows - 1, 0)[HALO:HALO + half]
        return (prev * cw_ref[0:1, cols] + h[HALO:HALO + half] * cw_ref[1:2, cols] + nxt * cw_ref[2:3, cols]
                + cb_ref[:, cols])

    def mlp_matmuls(s):
        lhs = lhs_ref[s]
        for f in range(d_ff // FF_BLOCK):
            cols_a = slice(f * FF_BLOCK, (f + 1) * FF_BLOCK)
            cols_g = slice(d_ff + f * FF_BLOCK, d_ff + (f + 1) * FF_BLOCK)
            a = conv(jnp.dot(lhs, wup_ref[:, cols_a], preferred_element_type=jnp.float32), cols_a)
            gt = conv(jnp.dot(lhs, wup_ref[:, cols_g], preferred_element_type=jnp.float32), cols_g)
            act_ref[s, :, cols_a] = (gt * (1.0 / (1.0 + jnp.exp(-gt))) * a).astype(bf)
        return jnp.dot(act_ref[s], wd_ref[...], preferred_element_type=jnp.float32)

    def final_norm(s, f_out):
        out_ref[0, s * half:(s + 1) * half, :] = _layer_norm(DEEPNORM_ALPHA * x1_ref[s] + f_out,
                                                             g2_ref[...], b2_ref[...])

    first_stage(0)
    first_stage(1)
    f_out0 = mlp_matmuls(0)
    f_out1 = mlp_matmuls(1)
    final_norm(0, f_out0)
    final_norm(1, f_out1)


def _tail(x, og, on, w_out, gi, bi, g1, b1, w_up, conv_w, conv_b, w_down, g2, b2):
    B, T, D = x.shape
    tm = TOK_TILE
    d_ff = w_down.shape[0]
    eb = tm // EDGE
    last_eb = T // EDGE - 1
    tok = lambda n: pl.BlockSpec((1, tm, n), lambda b_, t: (b_, t, 0))
    top = lambda n: pl.BlockSpec((1, EDGE, n), lambda b_, t: (b_, jnp.maximum(t * eb - 1, 0), 0))
    bot = lambda n: pl.BlockSpec((1, EDGE, n), lambda b_, t: (b_, jnp.minimum((t + 1) * eb, last_eb), 0))
    full = lambda a: pl.BlockSpec(a.shape, lambda b_, t: (0,) * a.ndim)
    resident = lambda a: pl.BlockSpec(a.shape, lambda b_, t: (0,) * a.ndim, pipeline_mode=pl.Buffered(1))
    return pl.pallas_call(
        _tail_kernel,
        grid=(B, T // tm),
        in_specs=[tok(D), top(D), bot(D), tok(512), top(512), bot(512), tok(512), top(512), bot(512),
                  resident(w_out), full(gi), full(bi), full(g1), full(b1),
                  resident(w_up), full(conv_w), full(conv_b), resident(w_down), full(g2), full(b2)],
        out_specs=tok(D),
        out_shape=jax.ShapeDtypeStruct((B, T, D), jnp.float32),
        scratch_shapes=[pltpu.VMEM((2, tm // 2, D), jnp.float32),
                        pltpu.VMEM((2, tm // 2 + 2 * HALO, D), jnp.bfloat16),
                        pltpu.VMEM((2, tm // 2, d_ff), jnp.bfloat16)],
        compiler_params=pltpu.CompilerParams(dimension_semantics=("parallel", "parallel"),
                                             vmem_limit_bytes=MLP_VMEM_LIMIT_BYTES),
        name="tail",
    )(x, x, x, og, og, og, on, on, on, w_out, gi, bi, g1, b1, w_up, conv_w, conv_b, w_down, g2, b2)


def _prepare_weights(ln_in_g, ln_in_b, w_in, gw_f, gb_f, gw_b, gb_b, gla_norm_g, na_rpb, w_out,
                     ln1_g, ln1_b, w_up, conv_w, conv_b, w_down, ln2_g, ln2_b):
    bf = jnp.bfloat16
    row = lambda a: a.reshape(1, -1).astype(jnp.float32)
    w = w_in[0]
    sizes = (256, 256, 512, 512, 32, 512, 512, 512)
    offs = np.concatenate([[0], np.cumsum(sizes)])
    gq, gk, gv, gg, glr, nq, nk, nv = [w[:, offs[i]:offs[i + 1]] for i in range(8)]
    lr_pad = jnp.zeros((w.shape[0], LANES - 2 * GLA_LOWRANK), w.dtype)
    wa = jnp.concatenate([gq * (GLA_DK ** -0.5), gk, gg, glr, lr_pad, nk], axis=1).astype(bf)
    wt = jnp.concatenate([gv, nq * (NA_DH ** -0.5), nv], axis=1).T.astype(bf)
    zpad = jnp.zeros((LANES - 2 * GLA_LOWRANK, gw_f.shape[-1]), jnp.float32)
    zlr = jnp.zeros((GLA_LOWRANK, gw_f.shape[-1]), jnp.float32)
    wg_f = jnp.concatenate([gw_f[0], zlr, zpad], axis=0).astype(bf)
    wg_b = jnp.concatenate([zlr, gw_b[0], zpad], axis=0).astype(bf)
    bias = _na_bias_table(na_rpb[0])
    nty, nh, nq, nk = bias.shape
    bias = bias.reshape(nty, nh // 2, 2, nq, nk).transpose(0, 1, 4, 2, 3).reshape(nty, nh // 2, nk, 2 * nq)
    return dict(
        ln_in=(row(ln_in_g), row(ln_in_b)), wa=wa, wt=wt,
        wg_f=wg_f, bg_f=row(gb_f[0]), wg_b=wg_b, bg_b=row(gb_b[0]), ng=row(gla_norm_g[0]),
        tri_f=_cumsum_operator(False), tri_b=_cumsum_operator(True),
        bias=bias, w_out=w_out[0].astype(bf), ln1=(row(ln1_g[0]), row(ln1_b[0])),
        w_up=w_up[0].astype(bf), conv_w=conv_w[0].astype(jnp.float32), conv_b=row(conv_b[0]),
        w_down=w_down[0].astype(bf), ln2=(row(ln2_g[0]), row(ln2_b[0])),
    )


def _trunk(x, p):
    gq, gk, gs, lr, nk, gvt, nqt, nvt = _in_proj(x, *p["ln_in"], p["wa"], p["wt"])
    o_bwd = _gla(gq, gk, gvt, lr, p["wg_b"], p["bg_b"], p["tri_b"], reverse=True)
    o_gla = _gla(gq, gk, gvt, lr, p["wg_f"], p["bg_f"], p["tri_f"], reverse=False, gs=gs, ob=o_bwd, ng=p["ng"])
    o_na = _natten(nqt, nk, nvt, p["bias"])
    return _tail(x, o_gla, o_na, p["w_out"], *p["ln_in"], *p["ln1"],
                 p["w_up"], p["conv_w"], p["conv_b"], p["w_down"], *p["ln2"])


def kernel(x_prompt, x_sample, ln_in_g, ln_in_b, w_in, gla_gate_w_fwd, gla_gate_b_fwd, gla_gate_w_bwd,
           gla_gate_b_bwd, gla_norm_g, na_rpb, w_out, ln1_g, ln1_b, w_up, conv_w, conv_b, w_down, ln2_g, ln2_b):
    p = _prepare_weights(ln_in_g, ln_in_b, w_in, gla_gate_w_fwd, gla_gate_b_fwd, gla_gate_w_bwd,
                         gla_gate_b_bwd, gla_norm_g, na_rpb, w_out, ln1_g, ln1_b, w_up, conv_w, conv_b,
                         w_down, ln2_g, ln2_b)
    return (_trunk(x_prompt, p), _trunk(x_sample, p))
```

```python
import functools

import numpy as np
import jax
import jax.numpy as jnp
from jax import lax
from jax.experimental import pallas as pl
from jax.experimental.pallas import tpu as pltpu

GRID_W = 64
GLA_HEADS, GLA_DK, GLA_DV, GLA_LOWRANK, GLA_TAU, GLA_CHUNK = 4, 64, 128, 16, 16.0, 64
NA_HEADS, NA_DH, NA_WIN_R, NA_WIN_C = 8, 64, 8, 16
LN_EPS, RMS_EPS = 1e-5, 1e-6
DEPTH = 1
DEEPNORM_ALPHA = (2.0 * DEPTH) ** 0.25

LANES = 128
SUBLANES_F32 = 8
SUBLANES_BF16 = 16
VMEM_LIMIT_BYTES = 48 * 1024 * 1024
MLP_VMEM_LIMIT_BYTES = 56 * 1024 * 1024

TOK_TILE = 512
GLA_BLOCK = 1024
SUPER = 2 * GLA_CHUNK
GATE_AHEAD = 2
NA_QROWS = 2
NA_WBLOCKS = 5
NA_PAIRS = 4
FF_BLOCK = 256
HALO = SUBLANES_F32
EDGE = SUBLANES_BF16

_NT = (((1,), (1,)), ((), ()))


def _layer_norm(x, g, b):
    mu = jnp.mean(x, axis=-1, keepdims=True)
    xc = x - mu
    var = jnp.mean(xc * xc, axis=-1, keepdims=True)
    return xc * lax.rsqrt(var + LN_EPS) * g + b


def _cparams(sem):
    return pltpu.CompilerParams(dimension_semantics=sem, vmem_limit_bytes=VMEM_LIMIT_BYTES)


_A_GQ, _A_GK, _A_GG, _A_LR, _A_NQ, _A_NV, _A_END = 0, 256, 512, 1024, 1152, 1664, 2176


def _in_proj_kernel(x_ref, g_ref, b_ref, wa_ref, wt_ref,
                    gq_ref, gk_ref, gs_ref, lr_ref, nq_ref, nv_ref, gvt_ref, nkt_ref):
    tm = x_ref.shape[1]
    half = tm // 2
    xn = [_layer_norm(x_ref[0, s * half:(s + 1) * half, :], g_ref[...], b_ref[...]).astype(jnp.bfloat16)
          for s in range(2)]
    for s in range(2):
        rows = slice(s * half, (s + 1) * half)
        for lo, hi, out_ref in ((_A_GQ, _A_GK, gq_ref), (_A_GK, _A_GG, gk_ref), (_A_GG, _A_LR, gs_ref),
                                (_A_LR, _A_NQ, lr_ref), (_A_NQ, _A_NV, nq_ref), (_A_NV, _A_END, nv_ref)):
            y = jnp.dot(xn[s], wa_ref[:, lo:hi], preferred_element_type=jnp.float32)
            if out_ref is gs_ref:
                y = y * (1.0 / (1.0 + jnp.exp(-y)))
            out_ref[0, rows, :] = y.astype(out_ref.dtype)
        gvt_ref[0, :, rows] = lax.dot_general(wt_ref[0:512, :], xn[s], _NT,
                                              preferred_element_type=jnp.float32).astype(gvt_ref.dtype)
        nkt_ref[0, :, rows] = lax.dot_general(wt_ref[512:1024, :], xn[s], _NT,
                                              preferred_element_type=jnp.float32).astype(nkt_ref.dtype)


def _in_proj(x, g, b, wa, wt):
    B, T, D = x.shape
    tm = TOK_TILE
    tok = lambda n: pl.BlockSpec((1, tm, n), lambda bi, ti: (bi, ti, 0))
    tr = pl.BlockSpec((1, 512, tm), lambda bi, ti: (bi, 0, ti))
    full = lambda a: pl.BlockSpec(a.shape, lambda bi, ti: (0,) * a.ndim)
    bf = jnp.bfloat16
    out_shape = (
        jax.ShapeDtypeStruct((B, T, 256), bf), jax.ShapeDtypeStruct((B, T, 256), bf),
        jax.ShapeDtypeStruct((B, T, 512), bf), jax.ShapeDtypeStruct((B, T, 128), bf),
        jax.ShapeDtypeStruct((B, T, 512), bf), jax.ShapeDtypeStruct((B, T, 512), bf),
        jax.ShapeDtypeStruct((B, 512, T), bf), jax.ShapeDtypeStruct((B, 512, T), bf),
    )
    return pl.pallas_call(
        _in_proj_kernel,
        grid=(B, T // tm),
        in_specs=[tok(D), full(g), full(b), full(wa), full(wt)],
        out_specs=(tok(256), tok(256), tok(512), tok(128), tok(512), tok(512), tr, tr),
        out_shape=out_shape,
        compiler_params=_cparams(("parallel", "parallel")),
        name="in_proj",
    )(x, g, b, wa, wt)


def _log_sigmoid(z):
    return jnp.minimum(z, 0.0) - jnp.log(1.0 + jnp.exp(-jnp.abs(z)))


def _gla_kernel(*refs, reverse, final):
    if final:
        (q_ref, k_ref, vt_ref, lr_ref, wg_ref, bg_ref, tri_ref, gs_ref, ob_ref, ng_ref,
         out_ref, st_ref) = refs
    else:
        (q_ref, k_ref, vt_ref, lr_ref, wg_ref, bg_ref, tri_ref, out_ref, st_ref) = refs
    C = GLA_CHUNK
    tb = q_ref.shape[1]
    nsc = tb // SUPER
    bf = jnp.bfloat16

    @pl.when(pl.program_id(1) == 0)
    def _():
        st_ref[...] = jnp.zeros_like(st_ref)

    tri = tri_ref[...]
    la_parts, bcums = {}, {}

    def gate_logits(sc):
        rows = slice(sc * SUPER, (sc + 1) * SUPER)
        z = jnp.dot(lr_ref[0, rows, :], wg_ref[...], preferred_element_type=jnp.float32) + bg_ref[...]
        la = _log_sigmoid(z) * (1.0 / GLA_TAU)
        la_hi = la.astype(bf)
        la_parts[sc] = (la_hi, (la - la_hi.astype(jnp.float32)).astype(bf))

    def gate_cumsum(sc):
        la_hi, la_lo = la_parts.pop(sc)
        bcums[sc] = (jnp.dot(tri, la_hi, preferred_element_type=jnp.float32)
                     + jnp.dot(tri, la_lo, preferred_element_type=jnp.float32))

    tot_row = 0 if reverse else C - 1

    lane = lax.broadcasted_iota(jnp.int32, (SUPER, LANES), 1)
    row = lax.broadcasted_iota(jnp.int32, (SUPER, LANES), 0)
    head_lane = [lane < GLA_DK, lane >= GLA_DK]
    same_chunk = (row < C) == (lane < C)
    amask = same_chunk & ((lane >= row) if reverse else (lane <= row))
    zc = jnp.zeros((C, LANES), bf)

    qm, a_sc, ut, dec, st_at = {}, {}, {}, {}, {}
    st = [st_ref[0], st_ref[1]]

    def products(sc):
        rows = slice(sc * SUPER, (sc + 1) * SUPER)
        bcum = bcums.pop(sc)
        for cc in range(2):
            dec[2 * sc + cc] = jnp.exp(bcum[cc * C + tot_row:cc * C + tot_row + 1, :])
        q_in = (q_ref[0, rows, :].astype(jnp.float32) * jnp.exp(bcum)).astype(bf)
        k_in32 = k_ref[0, rows, :].astype(jnp.float32) * jnp.exp(-bcum)
        k_in = k_in32.astype(bf)
        k_end = jnp.concatenate([(k_in32[cc * C:(cc + 1) * C] * dec[2 * sc + cc]).astype(bf) for cc in range(2)],
                                axis=0)
        for g in range(2):
            lanes = slice(g * LANES, (g + 1) * LANES)
            kin_sc, kend_sc = k_in[:, lanes], k_end[:, lanes]
            rhs = jnp.concatenate([jnp.concatenate([kend_sc[0:C], zc], axis=0),
                                   jnp.concatenate([zc, kend_sc[C:SUPER]], axis=0)], axis=1)
            ut_all = jnp.dot(vt_ref[0, g * 2 * GLA_DV:(g + 1) * 2 * GLA_DV, rows], rhs,
                             preferred_element_type=jnp.float32)
            for cc in range(2):
                cl = slice(cc * LANES, (cc + 1) * LANES)
                ut[2 * sc + cc, g] = jnp.where(head_lane[0], ut_all[0:GLA_DV, cl], ut_all[GLA_DV:2 * GLA_DV, cl])
            qp = q_in[:, lanes]
            for hh in range(2):
                qm[sc, 2 * g + hh] = jnp.where(head_lane[hh], qp, jnp.zeros_like(qp))
            a = lax.dot_general(jnp.concatenate([qm[sc, 2 * g], qm[sc, 2 * g + 1]], axis=0), kin_sc, _NT,
                                preferred_element_type=jnp.float32)
            for hh in range(2):
                a_sc[sc, 2 * g + hh] = jnp.where(amask, a[hh * SUPER:(hh + 1) * SUPER], 0.0).astype(bf)

    def advance_state(sc):
        for c in ((2 * sc + 1, 2 * sc) if reverse else (2 * sc, 2 * sc + 1)):
            for g in range(2):
                st_at[c, g] = st[g].astype(bf)
                st[g] = st[g] * dec[c][:, g * LANES:(g + 1) * LANES] + ut.pop((c, g))

    def outputs(sc):
        rows = slice(sc * SUPER, (sc + 1) * SUPER)
        for h in range(GLA_HEADS):
            g = h // 2
            q2 = qm[sc, h]
            lhs = jnp.concatenate([jnp.concatenate([q2[0:C], zc], axis=1),
                                   jnp.concatenate([zc, q2[C:SUPER]], axis=1)], axis=0)
            lhs = jnp.concatenate([lhs, a_sc[sc, h]], axis=1)
            rhs = jnp.concatenate([st_at[2 * sc, g], st_at[2 * sc + 1, g],
                                   vt_ref[0, h * GLA_DV:(h + 1) * GLA_DV, rows]], axis=1)
            o = lax.dot_general(lhs, rhs, _NT, preferred_element_type=jnp.float32)
            cols = slice(h * GLA_DV, (h + 1) * GLA_DV)
            if final:
                o = o + ob_ref[0, rows, cols]
                o = o * lax.rsqrt(jnp.mean(o * o, axis=-1, keepdims=True) + RMS_EPS) * ng_ref[...]
                o = o * gs_ref[0, rows, cols].astype(jnp.float32)
            out_ref[0, rows, cols] = o.astype(out_ref.dtype)

    order = list(range(nsc - 1, -1, -1) if reverse else range(nsc))
    ahead = nsc if final else GATE_AHEAD
    for sc in order[:ahead]:
        gate_logits(sc)
    for sc in order[:ahead - 1]:
        gate_cumsum(sc)
    for i, sc in enumerate(order):
        if i + ahead < nsc:
            gate_logits(order[i + ahead])
        if i + ahead - 1 < nsc:
            gate_cumsum(order[i + ahead - 1])
        products(sc)
        advance_state(sc)
        if i > 0:
            outputs(order[i - 1])
    outputs(order[-1])
    st_ref[0] = st[0]
    st_ref[1] = st[1]


def _gla(q, k, vt, lr, wg, bg, tri, *, reverse, gs=None, ob=None, ng=None):
    B, T, _ = q.shape
    tb = GLA_BLOCK
    nb = T // tb
    final = gs is not None
    blk = (lambda bi, i: (bi, nb - 1 - i, 0)) if reverse else (lambda bi, i: (bi, i, 0))
    blk_t = (lambda bi, i: (bi, 0, nb - 1 - i)) if reverse else (lambda bi, i: (bi, 0, i))
    tok = lambda n: pl.BlockSpec((1, tb, n), blk)
    full = lambda a: pl.BlockSpec(a.shape, lambda bi, i: (0,) * a.ndim)
    in_specs = [tok(256), tok(256), pl.BlockSpec((1, 512, tb), blk_t), tok(128), full(wg), full(bg), full(tri)]
    args = [q, k, vt, lr, wg, bg, tri]
    if final:
        in_specs += [tok(512), tok(512), full(ng)]
        args += [gs, ob, ng]
    return pl.pallas_call(
        functools.partial(_gla_kernel, reverse=reverse, final=final),
        grid=(B, nb),
        in_specs=in_specs,
        out_specs=tok(512),
        out_shape=jax.ShapeDtypeStruct((B, T, 512), jnp.bfloat16 if final else jnp.float32),
        scratch_shapes=[pltpu.VMEM((2, GLA_DV, 2 * GLA_DK), jnp.float32)],
        compiler_params=_cparams(("parallel", "arbitrary")),
        name="gla_fwd" if final else "gla_bwd",
    )(*args)


def _cumsum_operator(reverse):
    i = np.arange(SUPER)
    same = (i[:, None] // GLA_CHUNK) == (i[None, :] // GLA_CHUNK)
    tri = (i[None, :] >= i[:, None]) if reverse else (i[None, :] <= i[:, None])
    return jnp.asarray(same & tri, jnp.bfloat16)


def _na_window_block(p, npairs):
    return jnp.clip(p - 2, 0, npairs - NA_WBLOCKS)


def _na_pair_type(p, npairs):
    return jnp.where(p < 2, p, jnp.where(p >= npairs - 2, p - (npairs - 5), 2))


def _na_bias_table(rpb):
    rows = 64
    npairs = rows // 2
    wk = NA_WBLOCKS * NA_QROWS
    nh = rpb.shape[0]
    ring = jnp.concatenate([rpb[..., NA_WIN_C - 1:], jnp.zeros(rpb.shape[:2] + (LANES - (2 * NA_WIN_C - 1),), rpb.dtype),
                            rpb[..., :NA_WIN_C - 1]], axis=-1)
    toep = jnp.tile(ring, (1, 1, GRID_W))[..., :GRID_W * (LANES - 1)]
    toep = toep.reshape(nh, rpb.shape[1], GRID_W, LANES - 1)[..., :GRID_W]
    c = np.arange(GRID_W)
    qcs = np.clip(c - NA_WIN_C // 2, 0, GRID_W - NA_WIN_C)
    col_ok = (c[None, :] >= qcs[:, None]) & (c[None, :] < qcs[:, None] + NA_WIN_C)
    toep = jnp.where(col_ok[None, None], toep, -jnp.inf)
    toep = jnp.transpose(toep, (0, 2, 1, 3))
    types = []
    for p in (0, 1, 2, npairs - 2, npairs - 1):
        w0 = 2 * int(np.clip(p - 2, 0, npairs - NA_WBLOCKS))
        qrows = []
        for a in range(NA_QROWS):
            r = 2 * p + a
            rs = int(np.clip(r - NA_WIN_R // 2, 0, rows - NA_WIN_R))
            lo, first = rs - w0, rs - r + NA_WIN_R - 1
            valid = toep[:, :, first:first + NA_WIN_R, :].reshape(nh, GRID_W, NA_WIN_R * GRID_W)
            qrows.append(jnp.pad(valid, ((0, 0), (0, 0), (lo * GRID_W, (wk - NA_WIN_R - lo) * GRID_W)),
                                 constant_values=-jnp.inf))
        types.append(jnp.concatenate(qrows, axis=-2))
    return jnp.stack(types, axis=0).astype(jnp.float32)


def _natten_kernel(*refs, npairs):
    q_ref, bias_ref, out_ref = refs[0], refs[1], refs[-1]
    nq = NA_QROWS * GRID_W
    nk = nq * NA_WBLOCKS
    lane = lax.broadcasted_iota(jnp.int32, (nq, LANES), 1)
    head_lane = [lane < NA_DH, lane >= NA_DH]
    ones = jnp.ones((nk, LANES), jnp.bfloat16)
    pair_type = [_na_pair_type(NA_PAIRS * pl.program_id(1) + j, npairs) for j in range(NA_PAIRS)]

    def pair_refs(j):
        return refs[2 + 2 * j:4 + 2 * j]

    def qk(j, h):
        kt_ref, _ = pair_refs(j)
        g, hh = divmod(h, 2)
        lanes = slice(g * LANES, (g + 1) * LANES)
        qp = q_ref[0, j * nq:(j + 1) * nq, lanes]
        qm = jnp.where(head_lane[hh], qp, jnp.zeros_like(qp))
        return jnp.dot(qm, kt_ref[lanes, :], preferred_element_type=jnp.float32) + bias_ref[pair_type[j], h]

    def softmax_num(s):
        m = jnp.max(s, axis=-1, keepdims=True)
        return jnp.exp(s - m).astype(jnp.bfloat16)

    def pv(j, h, p):
        _, v_ref = pair_refs(j)
        lanes = slice((h // 2) * LANES, (h // 2 + 1) * LANES)
        v1 = jnp.concatenate([v_ref[:, lanes], ones], axis=1)
        o = jnp.dot(p, v1, preferred_element_type=jnp.float32)
        return o[:, 0:LANES] * (1.0 / o[:, LANES:2 * LANES])

    scores = {(j, h): qk(j, h) for j in range(NA_PAIRS) for h in range(NA_HEADS)}
    for j in range(NA_PAIRS):
        for g in range(NA_HEADS // 2):
            o0 = pv(j, 2 * g, softmax_num(scores.pop((j, 2 * g))))
            o1 = pv(j, 2 * g + 1, softmax_num(scores.pop((j, 2 * g + 1))))
            out_ref[0, j * nq:(j + 1) * nq, g * LANES:(g + 1) * LANES] = \
                jnp.where(head_lane[0], o0, o1).astype(out_ref.dtype)


def _natten(q, kt, v, bias):
    B, T, _ = q.shape
    nq = NA_QROWS * GRID_W
    npairs = T // nq
    q_spec = pl.BlockSpec((1, NA_PAIRS * nq, 512), lambda bi, s: (bi, s, 0))
    bias_spec = pl.BlockSpec(bias.shape, lambda bi, s: (0, 0, 0, 0), pipeline_mode=pl.Buffered(1))
    in_specs, args = [q_spec, bias_spec], [q, bias]
    nk = nq * NA_WBLOCKS
    for j in range(NA_PAIRS):
        pair = functools.partial(lambda s, j: NA_PAIRS * s + j, j=j)
        in_specs += [
            pl.BlockSpec((None, pl.Element(512), pl.Element(nk)), functools.partial(
                lambda bi, s, pair: (bi, 0, _na_window_block(pair(s), npairs) * nq), pair=pair)),
            pl.BlockSpec((None, pl.Element(nk), pl.Element(512)), functools.partial(
                lambda bi, s, pair: (bi, _na_window_block(pair(s), npairs) * nq, 0), pair=pair))]
        args += [kt, v]
    return pl.pallas_call(
        functools.partial(_natten_kernel, npairs=npairs),
        grid=(B, npairs // NA_PAIRS),
        in_specs=in_specs,
        out_specs=q_spec,
        out_shape=jax.ShapeDtypeStruct((B, T, 512), jnp.bfloat16),
        compiler_params=_cparams(("parallel", "arbitrary")),
        name="natten",
    )(*args)


def _tail_kernel(x_ref, xt_ref, xb_ref, og_ref, ogt_ref, ogb_ref, on_ref, ont_ref, onb_ref, wo_ref,
                 gi_ref, bi_ref, g1_ref, b1_ref, wup_ref, cw_ref, cb_ref, wd_ref, g2_ref, b2_ref, out_ref,
                 x1_ref, lhs_ref, act_ref):
    t = pl.program_id(1)
    tm = x_ref.shape[1]
    d_ff = wd_ref.shape[0]
    half = tm // 2
    rows = half + 2 * HALO
    bf = jnp.bfloat16

    def with_edges(s, ref, top_ref, bot_ref):
        if s == 0:
            return jnp.concatenate([top_ref[0], ref[0, 0:half + EDGE]], axis=0)
        return jnp.concatenate([ref[0, half - EDGE:tm], bot_ref[0]], axis=0)

    def first_stage(s):
        xn = _layer_norm(with_edges(s, x_ref, xt_ref, xb_ref), gi_ref[...], bi_ref[...])
        mix = (jnp.dot(with_edges(s, og_ref, ogt_ref, ogb_ref), wo_ref[0:512, :],
                       preferred_element_type=jnp.float32)
               + jnp.dot(with_edges(s, on_ref, ont_ref, onb_ref), wo_ref[512:1024, :],
                         preferred_element_type=jnp.float32))
        x1 = _layer_norm(DEEPNORM_ALPHA * xn + mix, g1_ref[...], b1_ref[...])
        x1_ref[s] = x1[EDGE:EDGE + half]
        top, bot = x1[EDGE - HALO:EDGE], x1[EDGE + half:EDGE + half + HALO]
        if s == 0:
            top = jnp.where(t > 0, top, 0.0)
        else:
            bot = jnp.where(t < pl.num_programs(1) - 1, bot, 0.0)
        lhs_ref[s] = jnp.concatenate([top, x1[EDGE:EDGE + half], bot], axis=0).astype(bf)

    def conv(h, cols):
        prev = pltpu.roll(h, 1, 0)[HALO:HALO + half]
        nxt = pltpu.roll(h, rows - 1, 0)[HALO:HALO + half]
        return (prev * cw_ref[0:1, cols] + h[HALO:HALO + half] * cw_ref[1:2, cols] + nxt * cw_ref[2:3, cols]
                + cb_ref[:, cols])

    def mlp_matmuls(s):
        lhs = lhs_ref[s]
        for f in range(d_ff // FF_BLOCK):
            cols_a = slice(f * FF_BLOCK, (f + 1) * FF_BLOCK)
            cols_g = slice(d_ff + f * FF_BLOCK, d_ff + (f + 1) * FF_BLOCK)
            a = conv(jnp.dot(lhs, wup_ref[:, cols_a], preferred_element_type=jnp.float32), cols_a)
            gt = conv(jnp.dot(lhs, wup_ref[:, cols_g], preferred_element_type=jnp.float32), cols_g)
            act_ref[s, :, cols_a] = (gt * (1.0 / (1.0 + jnp.exp(-gt))) * a).astype(bf)
        return jnp.dot(act_ref[s], wd_ref[...], preferred_element_type=jnp.float32)

    def final_norm(s, f_out):
        out_ref[0, s * half:(s + 1) * half, :] = _layer_norm(DEEPNORM_ALPHA * x1_ref[s] + f_out,
                                                             g2_ref[...], b2_ref[...])

    first_stage(0)
    first_stage(1)
    f_out0 = mlp_matmuls(0)
    f_out1 = mlp_matmuls(1)
    final_norm(0, f_out0)
    final_norm(1, f_out1)


def _tail(x, og, on, w_out, gi, bi, g1, b1, w_up, conv_w, conv_b, w_down, g2, b2):
    B, T, D = x.shape
    tm = TOK_TILE
    d_ff = w_down.shape[0]
    eb = tm // EDGE
    last_eb = T // EDGE - 1
    tok = lambda n: pl.BlockSpec((1, tm, n), lambda b_, t: (b_, t, 0))
    top = lambda n: pl.BlockSpec((1, EDGE, n), lambda b_, t: (b_, jnp.maximum(t * eb - 1, 0), 0))
    bot = lambda n: pl.BlockSpec((1, EDGE, n), lambda b_, t: (b_, jnp.minimum((t + 1) * eb, last_eb), 0))
    full = lambda a: pl.BlockSpec(a.shape, lambda b_, t: (0,) * a.ndim)
    resident = lambda a: pl.BlockSpec(a.shape, lambda b_, t: (0,) * a.ndim, pipeline_mode=pl.Buffered(1))
    return pl.pallas_call(
        _tail_kernel,
        grid=(B, T // tm),
        in_specs=[tok(D), top(D), bot(D), tok(512), top(512), bot(512), tok(512), top(512), bot(512),
                  resident(w_out), full(gi), full(bi), full(g1), full(b1),
                  resident(w_up), full(conv_w), full(conv_b), resident(w_down), full(g2), full(b2)],
        out_specs=tok(D),
        out_shape=jax.ShapeDtypeStruct((B, T, D), jnp.float32),
        scratch_shapes=[pltpu.VMEM((2, tm // 2, D), jnp.float32),
                        pltpu.VMEM((2, tm // 2 + 2 * HALO, D), jnp.bfloat16),
                        pltpu.VMEM((2, tm // 2, d_ff), jnp.bfloat16)],
        compiler_params=pltpu.CompilerParams(dimension_semantics=("parallel", "parallel"),
                                             vmem_limit_bytes=MLP_VMEM_LIMIT_BYTES),
        name="tail",
    )(x, x, x, og, og, og, on, on, on, w_out, gi, bi, g1, b1, w_up, conv_w, conv_b, w_down, g2, b2)


def _prepare_weights(ln_in_g, ln_in_b, w_in, gw_f, gb_f, gw_b, gb_b, gla_norm_g, na_rpb, w_out,
                     ln1_g, ln1_b, w_up, conv_w, conv_b, w_down, ln2_g, ln2_b):
    bf = jnp.bfloat16
    row = lambda a: a.reshape(1, -1).astype(jnp.float32)
    w = w_in[0]
    sizes = (256, 256, 512, 512, 32, 512, 512, 512)
    offs = np.concatenate([[0], np.cumsum(sizes)])
    gq, gk, gv, gg, glr, nq, nk, nv = [w[:, offs[i]:offs[i + 1]] for i in range(8)]
    lr_pad = jnp.zeros((w.shape[0], LANES - 2 * GLA_LOWRANK), w.dtype)
    wa = jnp.concatenate([gq * (GLA_DK ** -0.5), gk, gg, glr, lr_pad, nq * (NA_DH ** -0.5), nv], axis=1).astype(bf)
    wt = jnp.concatenate([gv, nk], axis=1).T.astype(bf)
    zpad = jnp.zeros((LANES - 2 * GLA_LOWRANK, gw_f.shape[-1]), jnp.float32)
    zlr = jnp.zeros((GLA_LOWRANK, gw_f.shape[-1]), jnp.float32)
    wg_f = jnp.concatenate([gw_f[0], zlr, zpad], axis=0).astype(bf)
    wg_b = jnp.concatenate([zlr, gw_b[0], zpad], axis=0).astype(bf)
    bias = _na_bias_table(na_rpb[0])
    return dict(
        ln_in=(row(ln_in_g), row(ln_in_b)), wa=wa, wt=wt,
        wg_f=wg_f, bg_f=row(gb_f[0]), wg_b=wg_b, bg_b=row(gb_b[0]), ng=row(gla_norm_g[0]),
        tri_f=_cumsum_operator(False), tri_b=_cumsum_operator(True),
        bias=bias, w_out=w_out[0].astype(bf), ln1=(row(ln1_g[0]), row(ln1_b[0])),
        w_up=w_up[0].astype(bf), conv_w=conv_w[0].astype(jnp.float32), conv_b=row(conv_b[0]),
        w_down=w_down[0].astype(bf), ln2=(row(ln2_g[0]), row(ln2_b[0])),
    )


def _trunk(x, p):
    gq, gk, gs, lr, nq, nv, gvt, nkt = _in_proj(x, *p["ln_in"], p["wa"], p["wt"])
    o_bwd = _gla(gq, gk, gvt, lr, p["wg_b"], p["bg_b"], p["tri_b"], reverse=True)
    o_gla = _gla(gq, gk, gvt, lr, p["wg_f"], p["bg_f"], p["tri_f"], reverse=False, gs=gs, ob=o_bwd, ng=p["ng"])
    o_na = _natten(nq, nkt, nv, p["bias"])
    return _tail(x, o_gla, o_na, p["w_out"], *p["ln_in"], *p["ln1"],
                 p["w_up"], p["conv_w"], p["conv_b"], p["w_down"], *p["ln2"])


def kernel(x_prompt, x_sample, ln_in_g, ln_in_b, w_in, gla_gate_w_fwd, gla_gate_b_fwd, gla_gate_w_bwd,
           gla_gate_b_bwd, gla_norm_g, na_rpb, w_out, ln1_g, ln1_b, w_up, conv_w, conv_b, w_down, ln2_g, ln2_b):
    p = _prepare_weights(ln_in_g, ln_in_b, w_in, gla_gate_w_fwd, gla_gate_b_fwd, gla_gate_w_bwd,
                         gla_gate_b_bwd, gla_norm_g, na_rpb, w_out, ln1_g, ln1_b, w_up, conv_w, conv_b,
                         w_down, ln2_g, ln2_b)
    return (_trunk(x_prompt, p), _trunk(x_sample, p))
```

```python
import functools

import numpy as np
import jax
import jax.numpy as jnp
from jax import lax
from jax.experimental import pallas as pl
from jax.experimental.pallas import tpu as pltpu

GRID_W = 64
GLA_HEADS, GLA_DK, GLA_DV, GLA_LOWRANK, GLA_TAU, GLA_CHUNK = 4, 64, 128, 16, 16.0, 64
NA_HEADS, NA_DH, NA_WIN_R, NA_WIN_C = 8, 64, 8, 16
LN_EPS, RMS_EPS = 1e-5, 1e-6
DEPTH = 1
DEEPNORM_ALPHA = (2.0 * DEPTH) ** 0.25

LANES = 128
SUBLANES_F32 = 8
SUBLANES_BF16 = 16
VMEM_LIMIT_BYTES = 48 * 1024 * 1024
MLP_VMEM_LIMIT_BYTES = 56 * 1024 * 1024

TOK_TILE = 512
GLA_BLOCK = 1024
SUPER = 2 * GLA_CHUNK
GATE_AHEAD = 2
NA_QROWS = 2
NA_WBLOCKS = 5
NA_PAIRS = 4
FF_BLOCK = 256
HALO = SUBLANES_F32
EDGE = SUBLANES_BF16

_NT = (((1,), (1,)), ((), ()))


def _layer_norm(x, g, b):
    mu = jnp.mean(x, axis=-1, keepdims=True)
    xc = x - mu
    var = jnp.mean(xc * xc, axis=-1, keepdims=True)
    return xc * lax.rsqrt(var + LN_EPS) * g + b


def _cparams(sem):
    return pltpu.CompilerParams(dimension_semantics=sem, vmem_limit_bytes=VMEM_LIMIT_BYTES)


_A_GQ, _A_GK, _A_GG, _A_LR, _A_NQ, _A_NV, _A_END = 0, 256, 512, 1024, 1152, 1664, 2176


def _in_proj_kernel(x_ref, g_ref, b_ref, wa_ref, wt_ref,
                    gq_ref, gk_ref, gs_ref, lr_ref, nq_ref, nv_ref, gvt_ref, nkt_ref):
    tm = x_ref.shape[1]
    half = tm // 2
    xn = [_layer_norm(x_ref[0, s * half:(s + 1) * half, :], g_ref[...], b_ref[...]).astype(jnp.bfloat16)
          for s in range(2)]
    for s in range(2):
        rows = slice(s * half, (s + 1) * half)
        for lo, hi, out_ref in ((_A_GQ, _A_GK, gq_ref), (_A_GK, _A_GG, gk_ref), (_A_GG, _A_LR, gs_ref),
                                (_A_LR, _A_NQ, lr_ref), (_A_NQ, _A_NV, nq_ref), (_A_NV, _A_END, nv_ref)):
            y = jnp.dot(xn[s], wa_ref[:, lo:hi], preferred_element_type=jnp.float32)
            if out_ref is gs_ref:
                y = y * (1.0 / (1.0 + jnp.exp(-y)))
            out_ref[0, rows, :] = y.astype(out_ref.dtype)
        gvt_ref[0, :, rows] = lax.dot_general(wt_ref[0:512, :], xn[s], _NT,
                                              preferred_element_type=jnp.float32).astype(gvt_ref.dtype)
        nkt_ref[0, :, rows] = lax.dot_general(wt_ref[512:1024, :], xn[s], _NT,
                                              preferred_element_type=jnp.float32).astype(nkt_ref.dtype)


def _in_proj(x, g, b, wa, wt):
    B, T, D = x.shape
    tm = TOK_TILE
    tok = lambda n: pl.BlockSpec((1, tm, n), lambda bi, ti: (bi, ti, 0))
    tr = pl.BlockSpec((1, 512, tm), lambda bi, ti: (bi, 0, ti))
    full = lambda a: pl.BlockSpec(a.shape, lambda bi, ti: (0,) * a.ndim)
    bf = jnp.bfloat16
    out_shape = (
        jax.ShapeDtypeStruct((B, T, 256), bf), jax.ShapeDtypeStruct((B, T, 256), bf),
        jax.ShapeDtypeStruct((B, T, 512), bf), jax.ShapeDtypeStruct((B, T, 128), bf),
        jax.ShapeDtypeStruct((B, T, 512), bf), jax.ShapeDtypeStruct((B, T, 512), bf),
        jax.ShapeDtypeStruct((B, 512, T), bf), jax.ShapeDtypeStruct((B, 512, T), bf),
    )
    return pl.pallas_call(
        _in_proj_kernel,
        grid=(B, T // tm),
        in_specs=[tok(D), full(g), full(b), full(wa), full(wt)],
        out_specs=(tok(256), tok(256), tok(512), tok(128), tok(512), tok(512), tr, tr),
        out_shape=out_shape,
        compiler_params=_cparams(("parallel", "parallel")),
        name="in_proj",
    )(x, g, b, wa, wt)


def _log_sigmoid(z):
    return jnp.minimum(z, 0.0) - jnp.log(1.0 + jnp.exp(-jnp.abs(z)))


def _gla_kernel(*refs, reverse, final):
    if final:
        (q_ref, k_ref, vt_ref, lr_ref, wg_ref, bg_ref, tri_ref, gs_ref, ob_ref, ng_ref,
         out_ref, st_ref) = refs
    else:
        (q_ref, k_ref, vt_ref, lr_ref, wg_ref, bg_ref, tri_ref, out_ref, st_ref) = refs
    C = GLA_CHUNK
    tb = q_ref.shape[1]
    nsc = tb // SUPER
    bf = jnp.bfloat16

    @pl.when(pl.program_id(1) == 0)
    def _():
        st_ref[...] = jnp.zeros_like(st_ref)

    tri = tri_ref[...]
    la_parts, bcums = {}, {}

    def gate_logits(sc):
        rows = slice(sc * SUPER, (sc + 1) * SUPER)
        z = jnp.dot(lr_ref[0, rows, :], wg_ref[...], preferred_element_type=jnp.float32) + bg_ref[...]
        la = _log_sigmoid(z) * (1.0 / GLA_TAU)
        la_hi = la.astype(bf)
        la_parts[sc] = (la_hi, (la - la_hi.astype(jnp.float32)).astype(bf))

    def gate_cumsum(sc):
        la_hi, la_lo = la_parts.pop(sc)
        bcums[sc] = (jnp.dot(tri, la_hi, preferred_element_type=jnp.float32)
                     + jnp.dot(tri, la_lo, preferred_element_type=jnp.float32))

    tot_row = 0 if reverse else C - 1

    lane = lax.broadcasted_iota(jnp.int32, (SUPER, LANES), 1)
    row = lax.broadcasted_iota(jnp.int32, (SUPER, LANES), 0)
    head_lane = [lane < GLA_DK, lane >= GLA_DK]
    same_chunk = (row < C) == (lane < C)
    amask = same_chunk & ((lane >= row) if reverse else (lane <= row))
    zc = jnp.zeros((C, LANES), bf)

    qm, a_sc, ut, dec, st_at = {}, {}, {}, {}, {}
    st = [st_ref[0], st_ref[1]]

    def products(sc):
        rows = slice(sc * SUPER, (sc + 1) * SUPER)
        bcum = bcums.pop(sc)
        for cc in range(2):
            dec[2 * sc + cc] = jnp.exp(bcum[cc * C + tot_row:cc * C + tot_row + 1, :])
        q_in = (q_ref[0, rows, :].astype(jnp.float32) * jnp.exp(bcum)).astype(bf)
        k_in32 = k_ref[0, rows, :].astype(jnp.float32) * jnp.exp(-bcum)
        k_in = k_in32.astype(bf)
        k_end = jnp.concatenate([(k_in32[cc * C:(cc + 1) * C] * dec[2 * sc + cc]).astype(bf) for cc in range(2)],
                                axis=0)
        for g in range(2):
            lanes = slice(g * LANES, (g + 1) * LANES)
            kin_sc, kend_sc = k_in[:, lanes], k_end[:, lanes]
            rhs = jnp.concatenate([jnp.concatenate([kend_sc[0:C], zc], axis=0),
                                   jnp.concatenate([zc, kend_sc[C:SUPER]], axis=0)], axis=1)
            ut_all = jnp.dot(vt_ref[0, g * 2 * GLA_DV:(g + 1) * 2 * GLA_DV, rows], rhs,
                             preferred_element_type=jnp.float32)
            for cc in range(2):
                cl = slice(cc * LANES, (cc + 1) * LANES)
                ut[2 * sc + cc, g] = jnp.where(head_lane[0], ut_all[0:GLA_DV, cl], ut_all[GLA_DV:2 * GLA_DV, cl])
            qp = q_in[:, lanes]
            for hh in range(2):
                qm[sc, 2 * g + hh] = jnp.where(head_lane[hh], qp, jnp.zeros_like(qp))
            a = lax.dot_general(jnp.concatenate([qm[sc, 2 * g], qm[sc, 2 * g + 1]], axis=0), kin_sc, _NT,
                                preferred_element_type=jnp.float32)
            for hh in range(2):
                a_sc[sc, 2 * g + hh] = jnp.where(amask, a[hh * SUPER:(hh + 1) * SUPER], 0.0).astype(bf)

    def advance_state(sc):
        for c in ((2 * sc + 1, 2 * sc) if reverse else (2 * sc, 2 * sc + 1)):
            for g in range(2):
                st_at[c, g] = st[g].astype(bf)
                st[g] = st[g] * dec[c][:, g * LANES:(g + 1) * LANES] + ut.pop((c, g))

    def outputs(sc):
        rows = slice(sc * SUPER, (sc + 1) * SUPER)
        for h in range(GLA_HEADS):
            g = h // 2
            q2 = qm[sc, h]
            lhs = jnp.concatenate([jnp.concatenate([q2[0:C], zc], axis=1),
                                   jnp.concatenate([zc, q2[C:SUPER]], axis=1)], axis=0)
            lhs = jnp.concatenate([lhs, a_sc[sc, h]], axis=1)
            rhs = jnp.concatenate([st_at[2 * sc, g], st_at[2 * sc + 1, g],
                                   vt_ref[0, h * GLA_DV:(h + 1) * GLA_DV, rows]], axis=1)
            o = lax.dot_general(lhs, rhs, _NT, preferred_element_type=jnp.float32)
            cols = slice(h * GLA_DV, (h + 1) * GLA_DV)
            if final:
                o = o + ob_ref[0, rows, cols]
                o = o * lax.rsqrt(jnp.mean(o * o, axis=-1, keepdims=True) + RMS_EPS) * ng_ref[...]
                o = o * gs_ref[0, rows, cols].astype(jnp.float32)
            out_ref[0, rows, cols] = o.astype(out_ref.dtype)

    order = list(range(nsc - 1, -1, -1) if reverse else range(nsc))
    ahead = nsc if final else GATE_AHEAD
    for sc in order[:ahead]:
        gate_logits(sc)
    for sc in order[:ahead - 1]:
        gate_cumsum(sc)
    for i, sc in enumerate(order):
        if i + ahead < nsc:
            gate_logits(order[i + ahead])
        if i + ahead - 1 < nsc:
            gate_cumsum(order[i + ahead - 1])
        products(sc)
        advance_state(sc)
        if i > 0:
            outputs(order[i - 1])
    outputs(order[-1])
    st_ref[0] = st[0]
    st_ref[1] = st[1]


def _gla(q, k, vt, lr, wg, bg, tri, *, reverse, gs=None, ob=None, ng=None):
    B, T, _ = q.shape
    tb = GLA_BLOCK
    nb = T // tb
    final = gs is not None
    blk = (lambda bi, i: (bi, nb - 1 - i, 0)) if reverse else (lambda bi, i: (bi, i, 0))
    blk_t = (lambda bi, i: (bi, 0, nb - 1 - i)) if reverse else (lambda bi, i: (bi, 0, i))
    tok = lambda n: pl.BlockSpec((1, tb, n), blk)
    full = lambda a: pl.BlockSpec(a.shape, lambda bi, i: (0,) * a.ndim)
    in_specs = [tok(256), tok(256), pl.BlockSpec((1, 512, tb), blk_t), tok(128), full(wg), full(bg), full(tri)]
    args = [q, k, vt, lr, wg, bg, tri]
    if final:
        in_specs += [tok(512), tok(512), full(ng)]
        args += [gs, ob, ng]
    return pl.pallas_call(
        functools.partial(_gla_kernel, reverse=reverse, final=final),
        grid=(B, nb),
        in_specs=in_specs,
        out_specs=tok(512),
        out_shape=jax.ShapeDtypeStruct((B, T, 512), jnp.bfloat16 if final else jnp.float32),
        scratch_shapes=[pltpu.VMEM((2, GLA_DV, 2 * GLA_DK), jnp.float32)],
        compiler_params=_cparams(("parallel", "arbitrary")),
        name="gla_fwd" if final else "gla_bwd",
    )(*args)


def _cumsum_operator(reverse):
    i = np.arange(SUPER)
    same = (i[:, None] // GLA_CHUNK) == (i[None, :] // GLA_CHUNK)
    tri = (i[None, :] >= i[:, None]) if reverse else (i[None, :] <= i[:, None])
    return jnp.asarray(same & tri, jnp.bfloat16)


def _na_window_block(p, npairs):
    return jnp.clip(p - 2, 0, npairs - NA_WBLOCKS)


def _na_pair_type(p, npairs):
    return jnp.where(p < 2, p, jnp.where(p >= npairs - 2, p - (npairs - 5), 2))


def _na_bias_table(rpb):
    rows = 64
    npairs = rows // 2
    wk = NA_WBLOCKS * NA_QROWS
    nh = rpb.shape[0]
    ring = jnp.concatenate([rpb[..., NA_WIN_C - 1:], jnp.zeros(rpb.shape[:2] + (LANES - (2 * NA_WIN_C - 1),), rpb.dtype),
                            rpb[..., :NA_WIN_C - 1]], axis=-1)
    toep = jnp.tile(ring, (1, 1, GRID_W))[..., :GRID_W * (LANES - 1)]
    toep = toep.reshape(nh, rpb.shape[1], GRID_W, LANES - 1)[..., :GRID_W]
    c = np.arange(GRID_W)
    qcs = np.clip(c - NA_WIN_C // 2, 0, GRID_W - NA_WIN_C)
    col_ok = (c[None, :] >= qcs[:, None]) & (c[None, :] < qcs[:, None] + NA_WIN_C)
    toep = jnp.where(col_ok[None, None], toep, -jnp.inf)
    masked = jnp.full((nh, GRID_W, GRID_W), -jnp.inf, rpb.dtype)
    types = []
    for p in (0, 1, 2, npairs - 2, npairs - 1):
        w0 = 2 * int(np.clip(p - 2, 0, npairs - NA_WBLOCKS))
        qrows = []
        for a in range(NA_QROWS):
            r = 2 * p + a
            rs = int(np.clip(r - NA_WIN_R // 2, 0, rows - NA_WIN_R))
            blocks = [toep[:, w0 + i - r + NA_WIN_R - 1] if rs <= w0 + i < rs + NA_WIN_R else masked
                      for i in range(wk)]
            qrows.append(jnp.concatenate(blocks, axis=-1))
        types.append(jnp.concatenate(qrows, axis=-2))
    return jnp.stack(types, axis=0).astype(jnp.float32)


def _natten_kernel(*refs, npairs):
    q_ref, bias_ref, out_ref = refs[0], refs[1], refs[-1]
    nq = NA_QROWS * GRID_W
    nk = nq * NA_WBLOCKS
    lane = lax.broadcasted_iota(jnp.int32, (nq, LANES), 1)
    head_lane = [lane < NA_DH, lane >= NA_DH]
    ones = jnp.ones((nk, LANES), jnp.bfloat16)
    pair_type = [_na_pair_type(NA_PAIRS * pl.program_id(1) + j, npairs) for j in range(NA_PAIRS)]

    def pair_refs(j):
        return refs[2 + 2 * j:4 + 2 * j]

    def qk(j, h):
        kt_ref, _ = pair_refs(j)
        g, hh = divmod(h, 2)
        lanes = slice(g * LANES, (g + 1) * LANES)
        qp = q_ref[0, j * nq:(j + 1) * nq, lanes]
        qm = jnp.where(head_lane[hh], qp, jnp.zeros_like(qp))
        return jnp.dot(qm, kt_ref[lanes, :], preferred_element_type=jnp.float32) + bias_ref[pair_type[j], h]

    def softmax_num(s):
        m = jnp.max(s, axis=-1, keepdims=True)
        return jnp.exp(s - m).astype(jnp.bfloat16)

    def pv(j, h, p):
        _, v_ref = pair_refs(j)
        lanes = slice((h // 2) * LANES, (h // 2 + 1) * LANES)
        v1 = jnp.concatenate([v_ref[:, lanes], ones], axis=1)
        o = jnp.dot(p, v1, preferred_element_type=jnp.float32)
        return o[:, 0:LANES] * (1.0 / o[:, LANES:2 * LANES])

    scores = {(j, h): qk(j, h) for j in range(NA_PAIRS) for h in range(NA_HEADS)}
    for j in range(NA_PAIRS):
        for g in range(NA_HEADS // 2):
            o0 = pv(j, 2 * g, softmax_num(scores.pop((j, 2 * g))))
            o1 = pv(j, 2 * g + 1, softmax_num(scores.pop((j, 2 * g + 1))))
            out_ref[0, j * nq:(j + 1) * nq, g * LANES:(g + 1) * LANES] = \
                jnp.where(head_lane[0], o0, o1).astype(out_ref.dtype)


def _natten(q, kt, v, bias):
    B, T, _ = q.shape
    nq = NA_QROWS * GRID_W
    npairs = T // nq
    q_spec = pl.BlockSpec((1, NA_PAIRS * nq, 512), lambda bi, s: (bi, s, 0))
    bias_spec = pl.BlockSpec(bias.shape, lambda bi, s: (0, 0, 0, 0), pipeline_mode=pl.Buffered(1))
    in_specs, args = [q_spec, bias_spec], [q, bias]
    nk = nq * NA_WBLOCKS
    for j in range(NA_PAIRS):
        pair = functools.partial(lambda s, j: NA_PAIRS * s + j, j=j)
        in_specs += [
            pl.BlockSpec((None, pl.Element(512), pl.Element(nk)), functools.partial(
                lambda bi, s, pair: (bi, 0, _na_window_block(pair(s), npairs) * nq), pair=pair)),
            pl.BlockSpec((None, pl.Element(nk), pl.Element(512)), functools.partial(
                lambda bi, s, pair: (bi, _na_window_block(pair(s), npairs) * nq, 0), pair=pair))]
        args += [kt, v]
    return pl.pallas_call(
        functools.partial(_natten_kernel, npairs=npairs),
        grid=(B, npairs // NA_PAIRS),
        in_specs=in_specs,
        out_specs=q_spec,
        out_shape=jax.ShapeDtypeStruct((B, T, 512), jnp.bfloat16),
        compiler_params=_cparams(("parallel", "arbitrary")),
        name="natten",
    )(*args)


def _tail_kernel(x_ref, xt_ref, xb_ref, og_ref, ogt_ref, ogb_ref, on_ref, ont_ref, onb_ref, wo_ref,
                 gi_ref, bi_ref, g1_ref, b1_ref, wup_ref, cw_ref, cb_ref, wd_ref, g2_ref, b2_ref, out_ref,
                 x1_ref, lhs_ref, act_ref):
    t = pl.program_id(1)
    tm = x_ref.shape[1]
    d_ff = wd_ref.shape[0]
    half = tm // 2
    rows = half + 2 * HALO
    bf = jnp.bfloat16

    def with_edges(s, ref, top_ref, bot_ref):
        if s == 0:
            return jnp.concatenate([top_ref[0], ref[0, 0:half + EDGE]], axis=0)
        return jnp.concatenate([ref[0, half - EDGE:tm], bot_ref[0]], axis=0)

    def first_stage(s):
        xn = _layer_norm(with_edges(s, x_ref, xt_ref, xb_ref), gi_ref[...], bi_ref[...])
        mix = (jnp.dot(with_edges(s, og_ref, ogt_ref, ogb_ref), wo_ref[0:512, :],
                       preferred_element_type=jnp.float32)
               + jnp.dot(with_edges(s, on_ref, ont_ref, onb_ref), wo_ref[512:1024, :],
                         preferred_element_type=jnp.float32))
        x1 = _layer_norm(DEEPNORM_ALPHA * xn + mix, g1_ref[...], b1_ref[...])
        x1_ref[s] = x1[EDGE:EDGE + half]
        top, bot = x1[EDGE - HALO:EDGE], x1[EDGE + half:EDGE + half + HALO]
        if s == 0:
            top = jnp.where(t > 0, top, 0.0)
        else:
            bot = jnp.where(t < pl.num_programs(1) - 1, bot, 0.0)
        lhs_ref[s] = jnp.concatenate([top, x1[EDGE:EDGE + half], bot], axis=0).astype(bf)

    def conv(h, cols):
        prev = pltpu.roll(h, 1, 0)[HALO:HALO + half]
        nxt = pltpu.roll(h, rows - 1, 0)[HALO:HALO + half]
        return (prev * cw_ref[0:1, cols] + h[HALO:HALO + half] * cw_ref[1:2, cols] + nxt * cw_ref[2:3, cols]
                + cb_ref[:, cols])

    def mlp_matmuls(s):
        lhs = lhs_ref[s]
        for f in range(d_ff // FF_BLOCK):
            cols_a = slice(f * FF_BLOCK, (f + 1) * FF_BLOCK)
            cols_g = slice(d_ff + f * FF_BLOCK, d_ff + (f + 1) * FF_BLOCK)
            a = conv(jnp.dot(lhs, wup_ref[:, cols_a], preferred_element_type=jnp.float32), cols_a)
            gt = conv(jnp.dot(lhs, wup_ref[:, cols_g], preferred_element_type=jnp.float32), cols_g)
            act_ref[s, :, cols_a] = (gt * (1.0 / (1.0 + jnp.exp(-gt))) * a).astype(bf)
        return jnp.dot(act_ref[s], wd_ref[...], preferred_element_type=jnp.float32)

    def final_norm(s, f_out):
        out_ref[0, s * half:(s + 1) * half, :] = _layer_norm(DEEPNORM_ALPHA * x1_ref[s] + f_out,
                                                             g2_ref[...], b2_ref[...])

    first_stage(0)
    first_stage(1)
    f_out0 = mlp_matmuls(0)
    f_out1 = mlp_matmuls(1)
    final_norm(0, f_out0)
    final_norm(1, f_out1)


def _tail(x, og, on, w_out, gi, bi, g1, b1, w_up, conv_w, conv_b, w_down, g2, b2):
    B, T, D = x.shape
    tm = TOK_TILE
    d_ff = w_down.shape[0]
    eb = tm // EDGE
    last_eb = T // EDGE - 1
    tok = lambda n: pl.BlockSpec((1, tm, n), lambda b_, t: (b_, t, 0))
    top = lambda n: pl.BlockSpec((1, EDGE, n), lambda b_, t: (b_, jnp.maximum(t * eb - 1, 0), 0))
    bot = lambda n: pl.BlockSpec((1, EDGE, n), lambda b_, t: (b_, jnp.minimum((t + 1) * eb, last_eb), 0))
    full = lambda a: pl.BlockSpec(a.shape, lambda b_, t: (0,) * a.ndim)
    resident = lambda a: pl.BlockSpec(a.shape, lambda b_, t: (0,) * a.ndim, pipeline_mode=pl.Buffered(1))
    return pl.pallas_call(
        _tail_kernel,
        grid=(B, T // tm),
        in_specs=[tok(D), top(D), bot(D), tok(512), top(512), bot(512), tok(512), top(512), bot(512),
                  resident(w_out), full(gi), full(bi), full(g1), full(b1),
                  resident(w_up), full(conv_w), full(conv_b), resident(w_down), full(g2), full(b2)],
        out_specs=tok(D),
        out_shape=jax.ShapeDtypeStruct((B, T, D), jnp.float32),
        scratch_shapes=[pltpu.VMEM((2, tm // 2, D), jnp.float32),
                        pltpu.VMEM((2, tm // 2 + 2 * HALO, D), jnp.bfloat16),
                        pltpu.VMEM((2, tm // 2, d_ff), jnp.bfloat16)],
        compiler_params=pltpu.CompilerParams(dimension_semantics=("parallel", "parallel"),
                                             vmem_limit_bytes=MLP_VMEM_LIMIT_BYTES),
        name="tail",
    )(x, x, x, og, og, og, on, on, on, w_out, gi, bi, g1, b1, w_up, conv_w, conv_b, w_down, g2, b2)


def _prepare_weights(ln_in_g, ln_in_b, w_in, gw_f, gb_f, gw_b, gb_b, gla_norm_g, na_rpb, w_out,
                     ln1_g, ln1_b, w_up, conv_w, conv_b, w_down, ln2_g, ln2_b):
    bf = jnp.bfloat16
    row = lambda a: a.reshape(1, -1).astype(jnp.float32)
    w = w_in[0]
    sizes = (256, 256, 512, 512, 32, 512, 512, 512)
    offs = np.concatenate([[0], np.cumsum(sizes)])
    gq, gk, gv, gg, glr, nq, nk, nv = [w[:, offs[i]:offs[i + 1]] for i in range(8)]
    lr_pad = jnp.zeros((w.shape[0], LANES - 2 * GLA_LOWRANK), w.dtype)
    wa = jnp.concatenate([gq * (GLA_DK ** -0.5), gk, gg, glr, lr_pad, nq * (NA_DH ** -0.5), nv], axis=1).astype(bf)
    wt = jnp.concatenate([gv, nk], axis=1).T.astype(bf)
    zpad = jnp.zeros((LANES - 2 * GLA_LOWRANK, gw_f.shape[-1]), jnp.float32)
    zlr = jnp.zeros((GLA_LOWRANK, gw_f.shape[-1]), jnp.float32)
    wg_f = jnp.concatenate([gw_f[0], zlr, zpad], axis=0).astype(bf)
    wg_b = jnp.concatenate([zlr, gw_b[0], zpad], axis=0).astype(bf)
    bias = _na_bias_table(na_rpb[0])
    return dict(
        ln_in=(row(ln_in_g), row(ln_in_b)), wa=wa, wt=wt,
        wg_f=wg_f, bg_f=row(gb_f[0]), wg_b=wg_b, bg_b=row(gb_b[0]), ng=row(gla_norm_g[0]),
        tri_f=_cumsum_operator(False), tri_b=_cumsum_operator(True),
        bias=bias, w_out=w_out[0].astype(bf), ln1=(row(ln1_g[0]), row(ln1_b[0])),
        w_up=w_up[0].astype(bf), conv_w=conv_w[0].astype(jnp.float32), conv_b=row(conv_b[0]),
        w_down=w_down[0].astype(bf), ln2=(row(ln2_g[0]), row(ln2_b[0])),
    )


def _trunk(x, p):
    gq, gk, gs, lr, nq, nv, gvt, nkt = _in_proj(x, *p["ln_in"], p["wa"], p["wt"])
    o_bwd = _gla(gq, gk, gvt, lr, p["wg_b"], p["bg_b"], p["tri_b"], reverse=True)
    o_gla = _gla(gq, gk, gvt, lr, p["wg_f"], p["bg_f"], p["tri_f"], reverse=False, gs=gs, ob=o_bwd, ng=p["ng"])
    o_na = _natten(nq, nkt, nv, p["bias"])
    return _tail(x, o_gla, o_na, p["w_out"], *p["ln_in"], *p["ln1"],
                 p["w_up"], p["conv_w"], p["conv_b"], p["w_down"], *p["ln2"])


def kernel(x_prompt, x_sample, ln_in_g, ln_in_b, w_in, gla_gate_w_fwd, gla_gate_b_fwd, gla_gate_w_bwd,
           gla_gate_b_bwd, gla_norm_g, na_rpb, w_out, ln1_g, ln1_b, w_up, conv_w, conv_b, w_down, ln2_g, ln2_b):
    p = _prepare_weights(ln_in_g, ln_in_b, w_in, gla_gate_w_fwd, gla_gate_b_fwd, gla_gate_w_bwd,
                         gla_gate_b_bwd, gla_norm_g, na_rpb, w_out, ln1_g, ln1_b, w_up, conv_w, conv_b,
                         w_down, ln2_g, ln2_b)
    return (_trunk(x_prompt, p), _trunk(x_sample, p))
```

```python
import functools

import numpy as np
import jax
import jax.numpy as jnp
from jax import lax
from jax.experimental import pallas as pl
from jax.experimental.pallas import tpu as pltpu

GRID_W = 64
GLA_HEADS, GLA_DK, GLA_DV, GLA_LOWRANK, GLA_TAU, GLA_CHUNK = 4, 64, 128, 16, 16.0, 64
NA_HEADS, NA_DH, NA_WIN_R, NA_WIN_C = 8, 64, 8, 16
LN_EPS, RMS_EPS = 1e-5, 1e-6
DEPTH = 1
DEEPNORM_ALPHA = (2.0 * DEPTH) ** 0.25
GLA_QK_W = GLA_HEADS * GLA_DK
GLA_V_W = GLA_HEADS * GLA_DV
NA_W = NA_HEADS * NA_DH

LANES = 128
SUBLANES_F32 = 8
SUBLANES_BF16 = 16
VMEM_LIMIT_BYTES = 48 * 1024 * 1024
MLP_VMEM_LIMIT_BYTES = 56 * 1024 * 1024

TOK_TILE = 512
TAIL_PARTS = 1
IN_TILE = 1024
GLA_BLOCK = 2048
GLA_BLOCK_FINAL = 1024
SUPER = 2 * GLA_CHUNK
GATE_AHEAD = 2
NA_QROWS = 2
NA_WBLOCKS = 5
NA_PAIRS = 4
FF_BLOCK = 256
HALO = SUBLANES_F32
EDGE = SUBLANES_BF16

_NT = (((1,), (1,)), ((), ()))


def _layer_norm(x, g, b):
    mu = jnp.mean(x, axis=-1, keepdims=True)
    xc = x - mu
    var = jnp.mean(xc * xc, axis=-1, keepdims=True)
    return xc * lax.rsqrt(var + LN_EPS) * g + b


def _cparams(sem):
    return pltpu.CompilerParams(dimension_semantics=sem, vmem_limit_bytes=VMEM_LIMIT_BYTES)


_A_GQ = 0
_A_GK = _A_GQ + GLA_QK_W
_A_GG = _A_GK + GLA_QK_W
_A_LR = _A_GG + GLA_V_W
_A_NQ = _A_LR + LANES
_A_NV = _A_NQ + NA_W
_A_END = _A_NV + NA_W


def _in_proj_kernel(x_ref, g_ref, b_ref, wa_ref, wt_ref,
                    gq_ref, gk_ref, gs_ref, lr_ref, nq_ref, nv_ref, gvt_ref, nkt_ref):
    tm = x_ref.shape[1]
    half = tm // 2
    xn = [_layer_norm(x_ref[0, s * half:(s + 1) * half, :], g_ref[...], b_ref[...]).astype(jnp.bfloat16)
          for s in range(2)]
    for s in range(2):
        rows = slice(s * half, (s + 1) * half)
        for lo, hi, out_ref in ((_A_GQ, _A_GK, gq_ref), (_A_GK, _A_GG, gk_ref), (_A_GG, _A_LR, gs_ref),
                                (_A_LR, _A_NQ, lr_ref), (_A_NQ, _A_NV, nq_ref), (_A_NV, _A_END, nv_ref)):
            y = jnp.dot(xn[s], wa_ref[:, lo:hi], preferred_element_type=jnp.float32)
            if out_ref is gs_ref:
                y = y * (1.0 / (1.0 + jnp.exp(-y)))
            out_ref[0, rows, :] = y.astype(out_ref.dtype)
        gvt_ref[0, :, rows] = lax.dot_general(wt_ref[0:GLA_V_W, :], xn[s], _NT,
                                              preferred_element_type=jnp.float32).astype(gvt_ref.dtype)
        nkt_ref[0, :, rows] = lax.dot_general(wt_ref[GLA_V_W:GLA_V_W + NA_W, :], xn[s], _NT,
                                              preferred_element_type=jnp.float32).astype(nkt_ref.dtype)


def _in_proj(x, g, b, wa, wt):
    B, T, D = x.shape
    tm = IN_TILE
    tok = lambda n: pl.BlockSpec((1, tm, n), lambda bi, ti: (bi, ti, 0))
    tr = lambda n: pl.BlockSpec((1, n, tm), lambda bi, ti: (bi, 0, ti))
    full = lambda a: pl.BlockSpec(a.shape, lambda bi, ti: (0,) * a.ndim)
    widths = (GLA_QK_W, GLA_QK_W, GLA_V_W, LANES, NA_W, NA_W)
    widths_t = (GLA_V_W, NA_W)
    out_shape = tuple([jax.ShapeDtypeStruct((B, T, n), jnp.bfloat16) for n in widths]
                      + [jax.ShapeDtypeStruct((B, n, T), jnp.bfloat16) for n in widths_t])
    return pl.pallas_call(
        _in_proj_kernel,
        grid=(B, T // tm),
        in_specs=[tok(D), full(g), full(b), full(wa), full(wt)],
        out_specs=tuple([tok(n) for n in widths] + [tr(n) for n in widths_t]),
        out_shape=out_shape,
        compiler_params=_cparams(("parallel", "parallel")),
        name="in_proj",
    )(x, g, b, wa, wt)


def _log_sigmoid(z):
    return jnp.minimum(z, 0.0) - jnp.log(1.0 + jnp.exp(-jnp.abs(z)))


def _gla_kernel(*refs, reverse, final):
    if final:
        (q_ref, k_ref, vt_ref, lr_ref, wg_ref, bg_ref, tri_ref, gs_ref, ob_ref, ng_ref,
         out_ref, st_ref) = refs
    else:
        (q_ref, k_ref, vt_ref, lr_ref, wg_ref, bg_ref, tri_ref, out_ref, st_ref) = refs
    C = GLA_CHUNK
    tb = q_ref.shape[1]
    nsc = tb // SUPER
    bf = jnp.bfloat16

    @pl.when(pl.program_id(1) == 0)
    def _():
        st_ref[...] = jnp.zeros_like(st_ref)

    tri = tri_ref[...]
    la_parts, bcums = {}, {}

    def gate_logits(sc):
        rows = slice(sc * SUPER, (sc + 1) * SUPER)
        z = jnp.dot(lr_ref[0, rows, :], wg_ref[...], preferred_element_type=jnp.float32) + bg_ref[...]
        la = _log_sigmoid(z) * (1.0 / GLA_TAU)
        la_hi = la.astype(bf)
        la_parts[sc] = (la_hi, (la - la_hi.astype(jnp.float32)).astype(bf))

    def gate_cumsum(sc):
        la_hi, la_lo = la_parts.pop(sc)
        bcums[sc] = (jnp.dot(tri, la_hi, preferred_element_type=jnp.float32)
                     + jnp.dot(tri, la_lo, preferred_element_type=jnp.float32))

    tot_row = 0 if reverse else C - 1

    lane = lax.broadcasted_iota(jnp.int32, (SUPER, LANES), 1)
    row = lax.broadcasted_iota(jnp.int32, (SUPER, LANES), 0)
    head_lane = [lane < GLA_DK, lane >= GLA_DK]
    same_chunk = (row < C) == (lane < C)
    amask = same_chunk & ((lane >= row) if reverse else (lane <= row))
    zc = jnp.zeros((C, LANES), bf)

    qm, a_sc, ut, dec, st_at = {}, {}, {}, {}, {}
    st = [st_ref[0], st_ref[1]]

    def products(sc):
        rows = slice(sc * SUPER, (sc + 1) * SUPER)
        bcum = bcums.pop(sc)
        for cc in range(2):
            dec[2 * sc + cc] = jnp.exp(bcum[cc * C + tot_row:cc * C + tot_row + 1, :])
        q_in = (q_ref[0, rows, :].astype(jnp.float32) * jnp.exp(bcum)).astype(bf)
        k_in32 = k_ref[0, rows, :].astype(jnp.float32) * jnp.exp(-bcum)
        k_in = k_in32.astype(bf)
        k_end = jnp.concatenate([(k_in32[cc * C:(cc + 1) * C] * dec[2 * sc + cc]).astype(bf) for cc in range(2)],
                                axis=0)
        for g in range(2):
            lanes = slice(g * LANES, (g + 1) * LANES)
            kin_sc, kend_sc = k_in[:, lanes], k_end[:, lanes]
            rhs = jnp.concatenate([jnp.concatenate([kend_sc[0:C], zc], axis=0),
                                   jnp.concatenate([zc, kend_sc[C:SUPER]], axis=0)], axis=1)
            ut_all = jnp.dot(vt_ref[0, g * 2 * GLA_DV:(g + 1) * 2 * GLA_DV, rows], rhs,
                             preferred_element_type=jnp.float32)
            for cc in range(2):
                cl = slice(cc * LANES, (cc + 1) * LANES)
                ut[2 * sc + cc, g] = jnp.where(head_lane[0], ut_all[0:GLA_DV, cl], ut_all[GLA_DV:2 * GLA_DV, cl])
            qp = q_in[:, lanes]
            for hh in range(2):
                qm[sc, 2 * g + hh] = jnp.where(head_lane[hh], qp, jnp.zeros_like(qp))
            a = lax.dot_general(jnp.concatenate([qm[sc, 2 * g], qm[sc, 2 * g + 1]], axis=0), kin_sc, _NT,
                                preferred_element_type=jnp.float32)
            for hh in range(2):
                a_sc[sc, 2 * g + hh] = jnp.where(amask, a[hh * SUPER:(hh + 1) * SUPER], 0.0).astype(bf)

    def advance_state(sc):
        for c in ((2 * sc + 1, 2 * sc) if reverse else (2 * sc, 2 * sc + 1)):
            for g in range(2):
                st_at[c, g] = st[g].astype(bf)
                st[g] = st[g] * dec[c][:, g * LANES:(g + 1) * LANES] + ut.pop((c, g))

    def outputs(sc):
        rows = slice(sc * SUPER, (sc + 1) * SUPER)
        for h in range(GLA_HEADS):
            g = h // 2
            q2 = qm[sc, h]
            lhs = jnp.concatenate([jnp.concatenate([q2[0:C], zc], axis=1),
                                   jnp.concatenate([zc, q2[C:SUPER]], axis=1)], axis=0)
            lhs = jnp.concatenate([lhs, a_sc[sc, h]], axis=1)
            rhs = jnp.concatenate([st_at[2 * sc, g], st_at[2 * sc + 1, g],
                                   vt_ref[0, h * GLA_DV:(h + 1) * GLA_DV, rows]], axis=1)
            o = lax.dot_general(lhs, rhs, _NT, preferred_element_type=jnp.float32)
            cols = slice(h * GLA_DV, (h + 1) * GLA_DV)
            if final:
                o = o + ob_ref[0, rows, cols]
                o = o * lax.rsqrt(jnp.mean(o * o, axis=-1, keepdims=True) + RMS_EPS) * ng_ref[...]
                o = o * gs_ref[0, rows, cols].astype(jnp.float32)
            out_ref[0, rows, cols] = o.astype(out_ref.dtype)

    order = list(range(nsc - 1, -1, -1) if reverse else range(nsc))
    ahead = nsc if final else GATE_AHEAD
    for sc in order[:ahead]:
        gate_logits(sc)
    for sc in order[:ahead - 1]:
        gate_cumsum(sc)
    for i, sc in enumerate(order):
        if i + ahead < nsc:
            gate_logits(order[i + ahead])
        if i + ahead - 1 < nsc:
            gate_cumsum(order[i + ahead - 1])
        products(sc)
        advance_state(sc)
        if i > 0:
            outputs(order[i - 1])
    outputs(order[-1])
    st_ref[0] = st[0]
    st_ref[1] = st[1]


def _gla(q, k, vt, lr, wg, bg, tri, *, reverse, gs=None, ob=None, ng=None):
    B, T, _ = q.shape
    final = gs is not None
    tb = GLA_BLOCK_FINAL if final else GLA_BLOCK
    nb = T // tb
    blk = (lambda bi, i: (bi, nb - 1 - i, 0)) if reverse else (lambda bi, i: (bi, i, 0))
    blk_t = (lambda bi, i: (bi, 0, nb - 1 - i)) if reverse else (lambda bi, i: (bi, 0, i))
    tok = lambda n: pl.BlockSpec((1, tb, n), blk)
    full = lambda a: pl.BlockSpec(a.shape, lambda bi, i: (0,) * a.ndim)
    in_specs = [tok(GLA_QK_W), tok(GLA_QK_W), pl.BlockSpec((1, GLA_V_W, tb), blk_t), tok(LANES),
                full(wg), full(bg), full(tri)]
    args = [q, k, vt, lr, wg, bg, tri]
    if final:
        in_specs += [tok(GLA_V_W), tok(GLA_V_W), full(ng)]
        args += [gs, ob, ng]
    return pl.pallas_call(
        functools.partial(_gla_kernel, reverse=reverse, final=final),
        grid=(B, nb),
        in_specs=in_specs,
        out_specs=tok(GLA_V_W),
        out_shape=jax.ShapeDtypeStruct((B, T, GLA_V_W), jnp.bfloat16 if final else jnp.float32),
        scratch_shapes=[pltpu.VMEM((2, GLA_DV, 2 * GLA_DK), jnp.float32)],
        compiler_params=_cparams(("parallel", "arbitrary")),
        name="gla_fwd" if final else "gla_bwd",
    )(*args)


def _cumsum_operator(reverse):
    i = np.arange(SUPER)
    same = (i[:, None] // GLA_CHUNK) == (i[None, :] // GLA_CHUNK)
    tri = (i[None, :] >= i[:, None]) if reverse else (i[None, :] <= i[:, None])
    return jnp.asarray(same & tri, jnp.bfloat16)


def _na_window_block(p, npairs):
    return jnp.clip(p - 2, 0, npairs - NA_WBLOCKS)


def _na_pair_type(p, npairs):
    return jnp.where(p < 2, p, jnp.where(p >= npairs - 2, p - (npairs - 5), 2))


def _na_bias_table(rpb):
    rows = 64
    npairs = rows // 2
    wk = NA_WBLOCKS * NA_QROWS
    nh = rpb.shape[0]
    ring = jnp.concatenate([rpb[..., NA_WIN_C - 1:], jnp.zeros(rpb.shape[:2] + (LANES - (2 * NA_WIN_C - 1),), rpb.dtype),
                            rpb[..., :NA_WIN_C - 1]], axis=-1)
    toep = jnp.tile(ring, (1, 1, GRID_W))[..., :GRID_W * (LANES - 1)]
    toep = toep.reshape(nh, rpb.shape[1], GRID_W, LANES - 1)[..., :GRID_W]
    c = np.arange(GRID_W)
    qcs = np.clip(c - NA_WIN_C // 2, 0, GRID_W - NA_WIN_C)
    col_ok = (c[None, :] >= qcs[:, None]) & (c[None, :] < qcs[:, None] + NA_WIN_C)
    toep = jnp.where(col_ok[None, None], toep, -jnp.inf)
    masked = jnp.full((nh, GRID_W, GRID_W), -jnp.inf, rpb.dtype)
    types = []
    for p in (0, 1, 2, npairs - 2, npairs - 1):
        w0 = 2 * int(np.clip(p - 2, 0, npairs - NA_WBLOCKS))
        qrows = []
        for a in range(NA_QROWS):
            r = 2 * p + a
            rs = int(np.clip(r - NA_WIN_R // 2, 0, rows - NA_WIN_R))
            blocks = [toep[:, w0 + i - r + NA_WIN_R - 1] if rs <= w0 + i < rs + NA_WIN_R else masked
                      for i in range(wk)]
            qrows.append(jnp.concatenate(blocks, axis=-1))
        types.append(jnp.concatenate(qrows, axis=-2))
    return jnp.stack(types, axis=0).astype(jnp.float32)


def _natten_kernel(*refs, npairs):
    q_ref, bias_ref, out_ref = refs[0], refs[1], refs[-1]
    nq = NA_QROWS * GRID_W
    nk = nq * NA_WBLOCKS
    lane = lax.broadcasted_iota(jnp.int32, (nq, LANES), 1)
    head_lane = [lane < NA_DH, lane >= NA_DH]
    ones = jnp.ones((nk, LANES), jnp.bfloat16)
    pair_type = [_na_pair_type(NA_PAIRS * pl.program_id(1) + j, npairs) for j in range(NA_PAIRS)]

    def pair_refs(j):
        return refs[2 + 2 * j:4 + 2 * j]

    def qk(j, h):
        kt_ref, _ = pair_refs(j)
        g, hh = divmod(h, 2)
        lanes = slice(g * LANES, (g + 1) * LANES)
        qp = q_ref[0, j * nq:(j + 1) * nq, lanes]
        qm = jnp.where(head_lane[hh], qp, jnp.zeros_like(qp))
        return jnp.dot(qm, kt_ref[lanes, :], preferred_element_type=jnp.float32) + bias_ref[pair_type[j], h]

    def softmax_num(s):
        m = jnp.max(s, axis=-1, keepdims=True)
        return jnp.exp(s - m).astype(jnp.bfloat16)

    def pv(j, h, p):
        _, v_ref = pair_refs(j)
        lanes = slice((h // 2) * LANES, (h // 2 + 1) * LANES)
        v1 = jnp.concatenate([v_ref[:, lanes], ones], axis=1)
        o = jnp.dot(p, v1, preferred_element_type=jnp.float32)
        return o[:, 0:LANES] * (1.0 / o[:, LANES:2 * LANES])

    scores = {(j, h): qk(j, h) for j in range(NA_PAIRS) for h in range(NA_HEADS)}
    for j in range(NA_PAIRS):
        for g in range(NA_HEADS // 2):
            o0 = pv(j, 2 * g, softmax_num(scores.pop((j, 2 * g))))
            o1 = pv(j, 2 * g + 1, softmax_num(scores.pop((j, 2 * g + 1))))
            out_ref[0, j * nq:(j + 1) * nq, g * LANES:(g + 1) * LANES] = \
                jnp.where(head_lane[0], o0, o1).astype(out_ref.dtype)


def _natten(q, kt, v, bias):
    B, T, _ = q.shape
    nq = NA_QROWS * GRID_W
    npairs = T // nq
    q_spec = pl.BlockSpec((1, NA_PAIRS * nq, NA_W), lambda bi, s: (bi, s, 0))
    bias_spec = pl.BlockSpec(bias.shape, lambda bi, s: (0, 0, 0, 0), pipeline_mode=pl.Buffered(1))
    in_specs, args = [q_spec, bias_spec], [q, bias]
    nk = nq * NA_WBLOCKS
    for j in range(NA_PAIRS):
        pair = functools.partial(lambda s, j: NA_PAIRS * s + j, j=j)
        in_specs += [
            pl.BlockSpec((None, pl.Element(NA_W), pl.Element(nk)), functools.partial(
                lambda bi, s, pair: (bi, 0, _na_window_block(pair(s), npairs) * nq), pair=pair)),
            pl.BlockSpec((None, pl.Element(nk), pl.Element(NA_W)), functools.partial(
                lambda bi, s, pair: (bi, _na_window_block(pair(s), npairs) * nq, 0), pair=pair))]
        args += [kt, v]
    return pl.pallas_call(
        functools.partial(_natten_kernel, npairs=npairs),
        grid=(B, npairs // NA_PAIRS),
        in_specs=in_specs,
        out_specs=q_spec,
        out_shape=jax.ShapeDtypeStruct((B, T, NA_W), jnp.bfloat16),
        compiler_params=_cparams(("parallel", "arbitrary")),
        name="natten",
    )(*args)


def _tail_kernel(x_ref, xt_ref, xb_ref, og_ref, ogt_ref, ogb_ref, on_ref, ont_ref, onb_ref, wo_ref,
                 gi_ref, bi_ref, g1_ref, b1_ref, wup_ref, cw_ref, cb_ref, wd_ref, g2_ref, b2_ref, out_ref,
                 x1_ref, lhs_ref, act_ref):
    t = pl.program_id(1)
    tm = x_ref.shape[1]
    d_ff = wd_ref.shape[0]
    half = tm // TAIL_PARTS
    rows = half + 2 * HALO
    bf = jnp.bfloat16

    def with_edges(s, ref, top_ref, bot_ref):
        lo, hi = max(half * s - EDGE, 0), min(half * (s + 1) + EDGE, tm)
        parts = ([top_ref[0]] if s == 0 else []) + [ref[0, lo:hi]] + ([bot_ref[0]] if s == TAIL_PARTS - 1 else [])
        return jnp.concatenate(parts, axis=0)

    def first_stage(s):
        xn = _layer_norm(with_edges(s, x_ref, xt_ref, xb_ref), gi_ref[...], bi_ref[...])
        mix = (jnp.dot(with_edges(s, og_ref, ogt_ref, ogb_ref), wo_ref[0:GLA_V_W, :],
                       preferred_element_type=jnp.float32)
               + jnp.dot(with_edges(s, on_ref, ont_ref, onb_ref), wo_ref[GLA_V_W:GLA_V_W + NA_W, :],
                         preferred_element_type=jnp.float32))
        x1 = _layer_norm(DEEPNORM_ALPHA * xn + mix, g1_ref[...], b1_ref[...])
        x1_ref[s] = x1[EDGE:EDGE + half]
        top, bot = x1[EDGE - HALO:EDGE], x1[EDGE + half:EDGE + half + HALO]
        if s == 0:
            top = jnp.where(t > 0, top, 0.0)
        if s == TAIL_PARTS - 1:
            bot = jnp.where(t < pl.num_programs(1) - 1, bot, 0.0)
        lhs_ref[s] = jnp.concatenate([top, x1[EDGE:EDGE + half], bot], axis=0).astype(bf)

    def conv(h, cols):
        prev = pltpu.roll(h, 1, 0)[HALO:HALO + half]
        nxt = pltpu.roll(h, rows - 1, 0)[HALO:HALO + half]
        return (prev * cw_ref[0:1, cols] + h[HALO:HALO + half] * cw_ref[1:2, cols] + nxt * cw_ref[2:3, cols]
                + cb_ref[:, cols])

    def mlp_matmuls(s):
        lhs = lhs_ref[s]
        for f in range(d_ff // FF_BLOCK):
            cols_a = slice(f * FF_BLOCK, (f + 1) * FF_BLOCK)
            cols_g = slice(d_ff + f * FF_BLOCK, d_ff + (f + 1) * FF_BLOCK)
            a = conv(jnp.dot(lhs, wup_ref[:, cols_a], preferred_element_type=jnp.float32), cols_a)
            gt = conv(jnp.dot(lhs, wup_ref[:, cols_g], preferred_element_type=jnp.float32), cols_g)
            act_ref[s, :, cols_a] = (gt * (1.0 / (1.0 + jnp.exp(-gt))) * a).astype(bf)
        return jnp.dot(act_ref[s], wd_ref[...], preferred_element_type=jnp.float32)

    def final_norm(s, f_out):
        out_ref[0, s * half:(s + 1) * half, :] = _layer_norm(DEEPNORM_ALPHA * x1_ref[s] + f_out,
                                                             g2_ref[...], b2_ref[...])

    for s in range(TAIL_PARTS):
        first_stage(s)
    f_outs = [mlp_matmuls(s) for s in range(TAIL_PARTS)]
    for s in range(TAIL_PARTS):
        final_norm(s, f_outs[s])


def _tail(x, og, on, w_out, gi, bi, g1, b1, w_up, conv_w, conv_b, w_down, g2, b2):
    B, T, D = x.shape
    tm = TOK_TILE
    d_ff = w_down.shape[0]
    eb = tm // EDGE
    last_eb = T // EDGE - 1
    tok = lambda n: pl.BlockSpec((1, tm, n), lambda b_, t: (b_, t, 0))
    top = lambda n: pl.BlockSpec((1, EDGE, n), lambda b_, t: (b_, jnp.maximum(t * eb - 1, 0), 0))
    bot = lambda n: pl.BlockSpec((1, EDGE, n), lambda b_, t: (b_, jnp.minimum((t + 1) * eb, last_eb), 0))
    full = lambda a: pl.BlockSpec(a.shape, lambda b_, t: (0,) * a.ndim)
    resident = lambda a: pl.BlockSpec(a.shape, lambda b_, t: (0,) * a.ndim, pipeline_mode=pl.Buffered(1))
    return pl.pallas_call(
        _tail_kernel,
        grid=(B, T // tm),
        in_specs=[tok(D), top(D), bot(D), tok(GLA_V_W), top(GLA_V_W), bot(GLA_V_W), tok(NA_W), top(NA_W), bot(NA_W),
                  resident(w_out), full(gi), full(bi), full(g1), full(b1),
                  resident(w_up), full(conv_w), full(conv_b), resident(w_down), full(g2), full(b2)],
        out_specs=tok(D),
        out_shape=jax.ShapeDtypeStruct((B, T, D), jnp.float32),
        scratch_shapes=[pltpu.VMEM((TAIL_PARTS, tm // TAIL_PARTS, D), jnp.float32),
                        pltpu.VMEM((TAIL_PARTS, tm // TAIL_PARTS + 2 * HALO, D), jnp.bfloat16),
                        pltpu.VMEM((TAIL_PARTS, tm // TAIL_PARTS, d_ff), jnp.bfloat16)],
        compiler_params=pltpu.CompilerParams(dimension_semantics=("parallel", "parallel"),
                                             vmem_limit_bytes=MLP_VMEM_LIMIT_BYTES),
        name="tail",
    )(x, x, x, og, og, og, on, on, on, w_out, gi, bi, g1, b1, w_up, conv_w, conv_b, w_down, g2, b2)


def _prepare_weights(ln_in_g, ln_in_b, w_in, gw_f, gb_f, gw_b, gb_b, gla_norm_g, na_rpb, w_out,
                     ln1_g, ln1_b, w_up, conv_w, conv_b, w_down, ln2_g, ln2_b):
    bf = jnp.bfloat16
    row = lambda a: a.reshape(1, -1).astype(jnp.float32)
    w = w_in[0]
    sizes = (GLA_QK_W, GLA_QK_W, GLA_V_W, GLA_V_W, 2 * GLA_LOWRANK, NA_W, NA_W, NA_W)
    offs = np.concatenate([[0], np.cumsum(sizes)])
    gq, gk, gv, gg, glr, nq, nk, nv = [w[:, offs[i]:offs[i + 1]] for i in range(8)]
    lr_pad = jnp.zeros((w.shape[0], LANES - 2 * GLA_LOWRANK), w.dtype)
    wa = jnp.concatenate([gq * (GLA_DK ** -0.5), gk, gg, glr, lr_pad, nq * (NA_DH ** -0.5), nv], axis=1).astype(bf)
    wt = jnp.concatenate([gv, nk], axis=1).T.astype(bf)
    zpad = jnp.zeros((LANES - 2 * GLA_LOWRANK, gw_f.shape[-1]), jnp.float32)
    zlr = jnp.zeros((GLA_LOWRANK, gw_f.shape[-1]), jnp.float32)
    wg_f = jnp.concatenate([gw_f[0], zlr, zpad], axis=0).astype(bf)
    wg_b = jnp.concatenate([zlr, gw_b[0], zpad], axis=0).astype(bf)
    bias = _na_bias_table(na_rpb[0])
    return dict(
        ln_in=(row(ln_in_g), row(ln_in_b)), wa=wa, wt=wt,
        wg_f=wg_f, bg_f=row(gb_f[0]), wg_b=wg_b, bg_b=row(gb_b[0]), ng=row(gla_norm_g[0]),
        tri_f=_cumsum_operator(False), tri_b=_cumsum_operator(True),
        bias=bias, w_out=w_out[0].astype(bf), ln1=(row(ln1_g[0]), row(ln1_b[0])),
        w_up=w_up[0].astype(bf), conv_w=conv_w[0].astype(jnp.float32), conv_b=row(conv_b[0]),
        w_down=w_down[0].astype(bf), ln2=(row(ln2_g[0]), row(ln2_b[0])),
    )


def _trunk(x, p):
    gq, gk, gs, lr, nq, nv, gvt, nkt = _in_proj(x, *p["ln_in"], p["wa"], p["wt"])
    o_bwd = _gla(gq, gk, gvt, lr, p["wg_b"], p["bg_b"], p["tri_b"], reverse=True)
    o_gla = _gla(gq, gk, gvt, lr, p["wg_f"], p["bg_f"], p["tri_f"], reverse=False, gs=gs, ob=o_bwd, ng=p["ng"])
    o_na = _natten(nq, nkt, nv, p["bias"])
    return _tail(x, o_gla, o_na, p["w_out"], *p["ln_in"], *p["ln1"],
                 p["w_up"], p["conv_w"], p["conv_b"], p["w_down"], *p["ln2"])


def kernel(x_prompt, x_sample, ln_in_g, ln_in_b, w_in, gla_gate_w_fwd, gla_gate_b_fwd, gla_gate_w_bwd,
           gla_gate_b_bwd, gla_norm_g, na_rpb, w_out, ln1_g, ln1_b, w_up, conv_w, conv_b, w_down, ln2_g, ln2_b):
    p = _prepare_weights(ln_in_g, ln_in_b, w_in, gla_gate_w_fwd, gla_gate_b_fwd, gla_gate_w_bwd,
                         gla_gate_b_bwd, gla_norm_g, na_rpb, w_out, ln1_g, ln1_b, w_up, conv_w, conv_b,
                         w_down, ln2_g, ln2_b)
    return (_trunk(x_prompt, p), _trunk(x_sample, p))
```

```python
import functools

import numpy as np
import jax
import jax.numpy as jnp
from jax import lax
from jax.experimental import pallas as pl
from jax.experimental.pallas import tpu as pltpu

GRID_W = 64
GLA_HEADS, GLA_DK, GLA_DV, GLA_LOWRANK, GLA_TAU, GLA_CHUNK = 4, 64, 128, 16, 16.0, 64
NA_HEADS, NA_DH, NA_WIN_R, NA_WIN_C = 8, 64, 8, 16
LN_EPS, RMS_EPS = 1e-5, 1e-6
DEPTH = 1
DEEPNORM_ALPHA = (2.0 * DEPTH) ** 0.25
GLA_QK_W = GLA_HEADS * GLA_DK
GLA_V_W = GLA_HEADS * GLA_DV
NA_W = NA_HEADS * NA_DH

LANES = 128
SUBLANES_F32 = 8
SUBLANES_BF16 = 16
VMEM_LIMIT_BYTES = 48 * 1024 * 1024
MLP_VMEM_LIMIT_BYTES = 62 * 1024 * 1024

TOK_TILE = 1024
TAIL_PARTS = 2
IN_TILE = 1024
GLA_BLOCK = 2048
GLA_BLOCK_FINAL = 1024
SUPER = 2 * GLA_CHUNK
GATE_AHEAD = 2
NA_QROWS = 2
NA_WBLOCKS = 5
NA_PAIRS = 4
FF_BLOCK = 256
HALO = SUBLANES_F32
EDGE = SUBLANES_BF16

_NT = (((1,), (1,)), ((), ()))


def _layer_norm(x, g, b):
    mu = jnp.mean(x, axis=-1, keepdims=True)
    xc = x - mu
    var = jnp.mean(xc * xc, axis=-1, keepdims=True)
    return xc * lax.rsqrt(var + LN_EPS) * g + b


def _cparams(sem):
    return pltpu.CompilerParams(dimension_semantics=sem, vmem_limit_bytes=VMEM_LIMIT_BYTES)


_A_GQ = 0
_A_GK = _A_GQ + GLA_QK_W
_A_GG = _A_GK + GLA_QK_W
_A_LR = _A_GG + GLA_V_W
_A_NQ = _A_LR + LANES
_A_NV = _A_NQ + NA_W
_A_END = _A_NV + NA_W


def _in_proj_kernel(x_ref, g_ref, b_ref, wa_ref, wt_ref,
                    gq_ref, gk_ref, gs_ref, lr_ref, nq_ref, nv_ref, gvt_ref, nkt_ref):
    tm = x_ref.shape[1]
    half = tm // 2
    xn = [_layer_norm(x_ref[0, s * half:(s + 1) * half, :], g_ref[...], b_ref[...]).astype(jnp.bfloat16)
          for s in range(2)]
    for s in range(2):
        rows = slice(s * half, (s + 1) * half)
        for lo, hi, out_ref in ((_A_GQ, _A_GK, gq_ref), (_A_GK, _A_GG, gk_ref), (_A_GG, _A_LR, gs_ref),
                                (_A_LR, _A_NQ, lr_ref), (_A_NQ, _A_NV, nq_ref), (_A_NV, _A_END, nv_ref)):
            y = jnp.dot(xn[s], wa_ref[:, lo:hi], preferred_element_type=jnp.float32)
            if out_ref is gs_ref:
                y = y * (1.0 / (1.0 + jnp.exp(-y)))
            out_ref[0, rows, :] = y.astype(out_ref.dtype)
        gvt_ref[0, :, rows] = lax.dot_general(wt_ref[0:GLA_V_W, :], xn[s], _NT,
                                              preferred_element_type=jnp.float32).astype(gvt_ref.dtype)
        nkt_ref[0, :, rows] = lax.dot_general(wt_ref[GLA_V_W:GLA_V_W + NA_W, :], xn[s], _NT,
                                              preferred_element_type=jnp.float32).astype(nkt_ref.dtype)


def _in_proj(x, g, b, wa, wt):
    B, T, D = x.shape
    tm = IN_TILE
    tok = lambda n: pl.BlockSpec((1, tm, n), lambda bi, ti: (bi, ti, 0))
    tr = lambda n: pl.BlockSpec((1, n, tm), lambda bi, ti: (bi, 0, ti))
    full = lambda a: pl.BlockSpec(a.shape, lambda bi, ti: (0,) * a.ndim)
    widths = (GLA_QK_W, GLA_QK_W, GLA_V_W, LANES, NA_W, NA_W)
    widths_t = (GLA_V_W, NA_W)
    out_shape = tuple([jax.ShapeDtypeStruct((B, T, n), jnp.bfloat16) for n in widths]
                      + [jax.ShapeDtypeStruct((B, n, T), jnp.bfloat16) for n in widths_t])
    return pl.pallas_call(
        _in_proj_kernel,
        grid=(B, T // tm),
        in_specs=[tok(D), full(g), full(b), full(wa), full(wt)],
        out_specs=tuple([tok(n) for n in widths] + [tr(n) for n in widths_t]),
        out_shape=out_shape,
        compiler_params=_cparams(("parallel", "parallel")),
        name="in_proj",
    )(x, g, b, wa, wt)


def _log_sigmoid(z):
    return jnp.minimum(z, 0.0) - jnp.log(1.0 + jnp.exp(-jnp.abs(z)))


def _gla_kernel(*refs, reverse, final):
    if final:
        (q_ref, k_ref, vt_ref, lr_ref, wg_ref, bg_ref, tri_ref, gs_ref, ob_ref, ng_ref,
         out_ref, st_ref) = refs
    else:
        (q_ref, k_ref, vt_ref, lr_ref, wg_ref, bg_ref, tri_ref, out_ref, st_ref) = refs
    C = GLA_CHUNK
    tb = q_ref.shape[1]
    nsc = tb // SUPER
    bf = jnp.bfloat16

    @pl.when(pl.program_id(1) == 0)
    def _():
        st_ref[...] = jnp.zeros_like(st_ref)

    tri = tri_ref[...]
    la_parts, bcums = {}, {}

    def gate_logits(sc):
        rows = slice(sc * SUPER, (sc + 1) * SUPER)
        z = jnp.dot(lr_ref[0, rows, :], wg_ref[...], preferred_element_type=jnp.float32) + bg_ref[...]
        la = _log_sigmoid(z) * (1.0 / GLA_TAU)
        la_hi = la.astype(bf)
        la_parts[sc] = (la_hi, (la - la_hi.astype(jnp.float32)).astype(bf))

    def gate_cumsum(sc):
        la_hi, la_lo = la_parts.pop(sc)
        bcums[sc] = (jnp.dot(tri, la_hi, preferred_element_type=jnp.float32)
                     + jnp.dot(tri, la_lo, preferred_element_type=jnp.float32))

    tot_row = 0 if reverse else C - 1

    lane = lax.broadcasted_iota(jnp.int32, (SUPER, LANES), 1)
    row = lax.broadcasted_iota(jnp.int32, (SUPER, LANES), 0)
    head_lane = [lane < GLA_DK, lane >= GLA_DK]
    same_chunk = (row < C) == (lane < C)
    amask = same_chunk & ((lane >= row) if reverse else (lane <= row))
    zc = jnp.zeros((C, LANES), bf)

    qm, a_sc, ut, dec, st_at = {}, {}, {}, {}, {}
    st = [st_ref[0], st_ref[1]]

    def products(sc):
        rows = slice(sc * SUPER, (sc + 1) * SUPER)
        bcum = bcums.pop(sc)
        for cc in range(2):
            dec[2 * sc + cc] = jnp.exp(bcum[cc * C + tot_row:cc * C + tot_row + 1, :])
        q_in = (q_ref[0, rows, :].astype(jnp.float32) * jnp.exp(bcum)).astype(bf)
        k_in32 = k_ref[0, rows, :].astype(jnp.float32) * jnp.exp(-bcum)
        k_in = k_in32.astype(bf)
        k_end = jnp.concatenate([(k_in32[cc * C:(cc + 1) * C] * dec[2 * sc + cc]).astype(bf) for cc in range(2)],
                                axis=0)
        for g in range(2):
            lanes = slice(g * LANES, (g + 1) * LANES)
            kin_sc, kend_sc = k_in[:, lanes], k_end[:, lanes]
            rhs = jnp.concatenate([jnp.concatenate([kend_sc[0:C], zc], axis=0),
                                   jnp.concatenate([zc, kend_sc[C:SUPER]], axis=0)], axis=1)
            ut_all = jnp.dot(vt_ref[0, g * 2 * GLA_DV:(g + 1) * 2 * GLA_DV, rows], rhs,
                             preferred_element_type=jnp.float32)
            for cc in range(2):
                cl = slice(cc * LANES, (cc + 1) * LANES)
                ut[2 * sc + cc, g] = jnp.where(head_lane[0], ut_all[0:GLA_DV, cl], ut_all[GLA_DV:2 * GLA_DV, cl])
            qp = q_in[:, lanes]
            for hh in range(2):
                qm[sc, 2 * g + hh] = jnp.where(head_lane[hh], qp, jnp.zeros_like(qp))
            a = lax.dot_general(jnp.concatenate([qm[sc, 2 * g], qm[sc, 2 * g + 1]], axis=0), kin_sc, _NT,
                                preferred_element_type=jnp.float32)
            for hh in range(2):
                a_sc[sc, 2 * g + hh] = jnp.where(amask, a[hh * SUPER:(hh + 1) * SUPER], 0.0).astype(bf)

    def advance_state(sc):
        for c in ((2 * sc + 1, 2 * sc) if reverse else (2 * sc, 2 * sc + 1)):
            for g in range(2):
                st_at[c, g] = st[g].astype(bf)
                st[g] = st[g] * dec[c][:, g * LANES:(g + 1) * LANES] + ut.pop((c, g))

    def outputs(sc):
        rows = slice(sc * SUPER, (sc + 1) * SUPER)
        for h in range(GLA_HEADS):
            g = h // 2
            q2 = qm[sc, h]
            lhs = jnp.concatenate([jnp.concatenate([q2[0:C], zc], axis=1),
                                   jnp.concatenate([zc, q2[C:SUPER]], axis=1)], axis=0)
            lhs = jnp.concatenate([lhs, a_sc[sc, h]], axis=1)
            rhs = jnp.concatenate([st_at[2 * sc, g], st_at[2 * sc + 1, g],
                                   vt_ref[0, h * GLA_DV:(h + 1) * GLA_DV, rows]], axis=1)
            o = lax.dot_general(lhs, rhs, _NT, preferred_element_type=jnp.float32)
            cols = slice(h * GLA_DV, (h + 1) * GLA_DV)
            if final:
                o = o + ob_ref[0, rows, cols]
                o = o * lax.rsqrt(jnp.mean(o * o, axis=-1, keepdims=True) + RMS_EPS) * ng_ref[...]
                o = o * gs_ref[0, rows, cols].astype(jnp.float32)
            out_ref[0, rows, cols] = o.astype(out_ref.dtype)

    order = list(range(nsc - 1, -1, -1) if reverse else range(nsc))
    ahead = nsc if final else GATE_AHEAD
    for sc in order[:ahead]:
        gate_logits(sc)
    for sc in order[:ahead - 1]:
        gate_cumsum(sc)
    for i, sc in enumerate(order):
        if i + ahead < nsc:
            gate_logits(order[i + ahead])
        if i + ahead - 1 < nsc:
            gate_cumsum(order[i + ahead - 1])
        products(sc)
        advance_state(sc)
        if i > 0:
            outputs(order[i - 1])
    outputs(order[-1])
    st_ref[0] = st[0]
    st_ref[1] = st[1]


def _gla(q, k, vt, lr, wg, bg, tri, *, reverse, gs=None, ob=None, ng=None):
    B, T, _ = q.shape
    final = gs is not None
    tb = GLA_BLOCK_FINAL if final else GLA_BLOCK
    nb = T // tb
    blk = (lambda bi, i: (bi, nb - 1 - i, 0)) if reverse else (lambda bi, i: (bi, i, 0))
    blk_t = (lambda bi, i: (bi, 0, nb - 1 - i)) if reverse else (lambda bi, i: (bi, 0, i))
    tok = lambda n: pl.BlockSpec((1, tb, n), blk)
    full = lambda a: pl.BlockSpec(a.shape, lambda bi, i: (0,) * a.ndim)
    in_specs = [tok(GLA_QK_W), tok(GLA_QK_W), pl.BlockSpec((1, GLA_V_W, tb), blk_t), tok(LANES),
                full(wg), full(bg), full(tri)]
    args = [q, k, vt, lr, wg, bg, tri]
    if final:
        in_specs += [tok(GLA_V_W), tok(GLA_V_W), full(ng)]
        args += [gs, ob, ng]
    return pl.pallas_call(
        functools.partial(_gla_kernel, reverse=reverse, final=final),
        grid=(B, nb),
        in_specs=in_specs,
        out_specs=tok(GLA_V_W),
        out_shape=jax.ShapeDtypeStruct((B, T, GLA_V_W), jnp.bfloat16 if final else jnp.float32),
        scratch_shapes=[pltpu.VMEM((2, GLA_DV, 2 * GLA_DK), jnp.float32)],
        compiler_params=_cparams(("parallel", "arbitrary")),
        name="gla_fwd" if final else "gla_bwd",
    )(*args)


def _cumsum_operator(reverse):
    i = np.arange(SUPER)
    same = (i[:, None] // GLA_CHUNK) == (i[None, :] // GLA_CHUNK)
    tri = (i[None, :] >= i[:, None]) if reverse else (i[None, :] <= i[:, None])
    return jnp.asarray(same & tri, jnp.bfloat16)


def _na_window_block(p, npairs):
    return jnp.clip(p - 2, 0, npairs - NA_WBLOCKS)


def _na_pair_type(p, npairs):
    return jnp.where(p < 2, p, jnp.where(p >= npairs - 2, p - (npairs - 5), 2))


def _na_bias_table(rpb):
    rows = 64
    npairs = rows // 2
    wk = NA_WBLOCKS * NA_QROWS
    nh = rpb.shape[0]
    ring = jnp.concatenate([rpb[..., NA_WIN_C - 1:], jnp.zeros(rpb.shape[:2] + (LANES - (2 * NA_WIN_C - 1),), rpb.dtype),
                            rpb[..., :NA_WIN_C - 1]], axis=-1)
    toep = jnp.tile(ring, (1, 1, GRID_W))[..., :GRID_W * (LANES - 1)]
    toep = toep.reshape(nh, rpb.shape[1], GRID_W, LANES - 1)[..., :GRID_W]
    c = np.arange(GRID_W)
    qcs = np.clip(c - NA_WIN_C // 2, 0, GRID_W - NA_WIN_C)
    col_ok = (c[None, :] >= qcs[:, None]) & (c[None, :] < qcs[:, None] + NA_WIN_C)
    toep = jnp.where(col_ok[None, None], toep, -jnp.inf)
    masked = jnp.full((nh, GRID_W, GRID_W), -jnp.inf, rpb.dtype)
    types = []
    for p in (0, 1, 2, npairs - 2, npairs - 1):
        w0 = 2 * int(np.clip(p - 2, 0, npairs - NA_WBLOCKS))
        qrows = []
        for a in range(NA_QROWS):
            r = 2 * p + a
            rs = int(np.clip(r - NA_WIN_R // 2, 0, rows - NA_WIN_R))
            blocks = [toep[:, w0 + i - r + NA_WIN_R - 1] if rs <= w0 + i < rs + NA_WIN_R else masked
                      for i in range(wk)]
            qrows.append(jnp.concatenate(blocks, axis=-1))
        types.append(jnp.concatenate(qrows, axis=-2))
    return jnp.stack(types, axis=0).astype(jnp.float32)


def _natten_kernel(*refs, npairs):
    q_ref, bias_ref, out_ref = refs[0], refs[1], refs[-1]
    nq = NA_QROWS * GRID_W
    nk = nq * NA_WBLOCKS
    lane = lax.broadcasted_iota(jnp.int32, (nq, LANES), 1)
    head_lane = [lane < NA_DH, lane >= NA_DH]
    ones = jnp.ones((nk, LANES), jnp.bfloat16)
    pair_type = [_na_pair_type(NA_PAIRS * pl.program_id(1) + j, npairs) for j in range(NA_PAIRS)]

    def pair_refs(j):
        return refs[2 + 2 * j:4 + 2 * j]

    def qk(j, h):
        kt_ref, _ = pair_refs(j)
        g, hh = divmod(h, 2)
        lanes = slice(g * LANES, (g + 1) * LANES)
        qp = q_ref[0, j * nq:(j + 1) * nq, lanes]
        qm = jnp.where(head_lane[hh], qp, jnp.zeros_like(qp))
        return jnp.dot(qm, kt_ref[lanes, :], preferred_element_type=jnp.float32) + bias_ref[pair_type[j], h]

    def softmax_num(s):
        m = jnp.max(s, axis=-1, keepdims=True)
        return jnp.exp(s - m).astype(jnp.bfloat16)

    def pv(j, h, p):
        _, v_ref = pair_refs(j)
        lanes = slice((h // 2) * LANES, (h // 2 + 1) * LANES)
        v1 = jnp.concatenate([v_ref[:, lanes], ones], axis=1)
        o = jnp.dot(p, v1, preferred_element_type=jnp.float32)
        return o[:, 0:LANES] * (1.0 / o[:, LANES:2 * LANES])

    scores = {(j, h): qk(j, h) for j in range(NA_PAIRS) for h in range(NA_HEADS)}
    for j in range(NA_PAIRS):
        for g in range(NA_HEADS // 2):
            o0 = pv(j, 2 * g, softmax_num(scores.pop((j, 2 * g))))
            o1 = pv(j, 2 * g + 1, softmax_num(scores.pop((j, 2 * g + 1))))
            out_ref[0, j * nq:(j + 1) * nq, g * LANES:(g + 1) * LANES] = \
                jnp.where(head_lane[0], o0, o1).astype(out_ref.dtype)


def _natten(q, kt, v, bias):
    B, T, _ = q.shape
    nq = NA_QROWS * GRID_W
    npairs = T // nq
    q_spec = pl.BlockSpec((1, NA_PAIRS * nq, NA_W), lambda bi, s: (bi, s, 0))
    bias_spec = pl.BlockSpec(bias.shape, lambda bi, s: (0, 0, 0, 0), pipeline_mode=pl.Buffered(1))
    in_specs, args = [q_spec, bias_spec], [q, bias]
    nk = nq * NA_WBLOCKS
    for j in range(NA_PAIRS):
        pair = functools.partial(lambda s, j: NA_PAIRS * s + j, j=j)
        in_specs += [
            pl.BlockSpec((None, pl.Element(NA_W), pl.Element(nk)), functools.partial(
                lambda bi, s, pair: (bi, 0, _na_window_block(pair(s), npairs) * nq), pair=pair)),
            pl.BlockSpec((None, pl.Element(nk), pl.Element(NA_W)), functools.partial(
                lambda bi, s, pair: (bi, _na_window_block(pair(s), npairs) * nq, 0), pair=pair))]
        args += [kt, v]
    return pl.pallas_call(
        functools.partial(_natten_kernel, npairs=npairs),
        grid=(B, npairs // NA_PAIRS),
        in_specs=in_specs,
        out_specs=q_spec,
        out_shape=jax.ShapeDtypeStruct((B, T, NA_W), jnp.bfloat16),
        compiler_params=_cparams(("parallel", "arbitrary")),
        name="natten",
    )(*args)


def _tail_kernel(x_ref, xt_ref, xb_ref, og_ref, ogt_ref, ogb_ref, on_ref, ont_ref, onb_ref, wo_ref,
                 gi_ref, bi_ref, g1_ref, b1_ref, wup_ref, cw_ref, cb_ref, wd_ref, g2_ref, b2_ref, out_ref,
                 x1_ref, lhs_ref, act_ref):
    t = pl.program_id(1)
    tm = x_ref.shape[1]
    d_ff = wd_ref.shape[0]
    half = tm // TAIL_PARTS
    rows = half + 2 * HALO
    bf = jnp.bfloat16

    def with_edges(s, ref, top_ref, bot_ref):
        lo, hi = max(half * s - EDGE, 0), min(half * (s + 1) + EDGE, tm)
        parts = ([top_ref[0]] if s == 0 else []) + [ref[0, lo:hi]] + ([bot_ref[0]] if s == TAIL_PARTS - 1 else [])
        return jnp.concatenate(parts, axis=0)

    def first_stage(s):
        xn = _layer_norm(with_edges(s, x_ref, xt_ref, xb_ref), gi_ref[...], bi_ref[...])
        mix = (jnp.dot(with_edges(s, og_ref, ogt_ref, ogb_ref), wo_ref[0:GLA_V_W, :],
                       preferred_element_type=jnp.float32)
               + jnp.dot(with_edges(s, on_ref, ont_ref, onb_ref), wo_ref[GLA_V_W:GLA_V_W + NA_W, :],
                         preferred_element_type=jnp.float32))
        x1 = _layer_norm(DEEPNORM_ALPHA * xn + mix, g1_ref[...], b1_ref[...])
        x1_ref[s] = x1[EDGE:EDGE + half]
        top, bot = x1[EDGE - HALO:EDGE], x1[EDGE + half:EDGE + half + HALO]
        if s == 0:
            top = jnp.where(t > 0, top, 0.0)
        if s == TAIL_PARTS - 1:
            bot = jnp.where(t < pl.num_programs(1) - 1, bot, 0.0)
        lhs_ref[s] = jnp.concatenate([top, x1[EDGE:EDGE + half], bot], axis=0).astype(bf)

    def conv(h, cols):
        prev = pltpu.roll(h, 1, 0)[HALO:HALO + half]
        nxt = pltpu.roll(h, rows - 1, 0)[HALO:HALO + half]
        return (prev * cw_ref[0:1, cols] + h[HALO:HALO + half] * cw_ref[1:2, cols] + nxt * cw_ref[2:3, cols]
                + cb_ref[:, cols])

    def mlp_matmuls(s):
        lhs = lhs_ref[s]
        for f in range(d_ff // FF_BLOCK):
            cols_a = slice(f * FF_BLOCK, (f + 1) * FF_BLOCK)
            cols_g = slice(d_ff + f * FF_BLOCK, d_ff + (f + 1) * FF_BLOCK)
            a = conv(jnp.dot(lhs, wup_ref[:, cols_a], preferred_element_type=jnp.float32), cols_a)
            gt = conv(jnp.dot(lhs, wup_ref[:, cols_g], preferred_element_type=jnp.float32), cols_g)
            act_ref[s, :, cols_a] = (gt * (1.0 / (1.0 + jnp.exp(-gt))) * a).astype(bf)
        return jnp.dot(act_ref[s], wd_ref[...], preferred_element_type=jnp.float32)

    def final_norm(s, f_out):
        out_ref[0, s * half:(s + 1) * half, :] = _layer_norm(DEEPNORM_ALPHA * x1_ref[s] + f_out,
                                                             g2_ref[...], b2_ref[...])

    for s in range(TAIL_PARTS):
        first_stage(s)
    f_outs = [mlp_matmuls(s) for s in range(TAIL_PARTS)]
    for s in range(TAIL_PARTS):
        final_norm(s, f_outs[s])


def _tail(x, og, on, w_out, gi, bi, g1, b1, w_up, conv_w, conv_b, w_down, g2, b2):
    B, T, D = x.shape
    tm = TOK_TILE
    d_ff = w_down.shape[0]
    eb = tm // EDGE
    last_eb = T // EDGE - 1
    tok = lambda n: pl.BlockSpec((1, tm, n), lambda b_, t: (b_, t, 0))
    top = lambda n: pl.BlockSpec((1, EDGE, n), lambda b_, t: (b_, jnp.maximum(t * eb - 1, 0), 0))
    bot = lambda n: pl.BlockSpec((1, EDGE, n), lambda b_, t: (b_, jnp.minimum((t + 1) * eb, last_eb), 0))
    full = lambda a: pl.BlockSpec(a.shape, lambda b_, t: (0,) * a.ndim)
    resident = lambda a: pl.BlockSpec(a.shape, lambda b_, t: (0,) * a.ndim, pipeline_mode=pl.Buffered(1))
    return pl.pallas_call(
        _tail_kernel,
        grid=(B, T // tm),
        in_specs=[tok(D), top(D), bot(D), tok(GLA_V_W), top(GLA_V_W), bot(GLA_V_W), tok(NA_W), top(NA_W), bot(NA_W),
                  resident(w_out), full(gi), full(bi), full(g1), full(b1),
                  resident(w_up), full(conv_w), full(conv_b), resident(w_down), full(g2), full(b2)],
        out_specs=tok(D),
        out_shape=jax.ShapeDtypeStruct((B, T, D), jnp.float32),
        scratch_shapes=[pltpu.VMEM((TAIL_PARTS, tm // TAIL_PARTS, D), jnp.float32),
                        pltpu.VMEM((TAIL_PARTS, tm // TAIL_PARTS + 2 * HALO, D), jnp.bfloat16),
                        pltpu.VMEM((TAIL_PARTS, tm // TAIL_PARTS, d_ff), jnp.bfloat16)],
        compiler_params=pltpu.CompilerParams(dimension_semantics=("parallel", "parallel"),
                                             vmem_limit_bytes=MLP_VMEM_LIMIT_BYTES),
        name="tail",
    )(x, x, x, og, og, og, on, on, on, w_out, gi, bi, g1, b1, w_up, conv_w, conv_b, w_down, g2, b2)


def _prepare_weights(ln_in_g, ln_in_b, w_in, gw_f, gb_f, gw_b, gb_b, gla_norm_g, na_rpb, w_out,
                     ln1_g, ln1_b, w_up, conv_w, conv_b, w_down, ln2_g, ln2_b):
    bf = jnp.bfloat16
    row = lambda a: a.reshape(1, -1).astype(jnp.float32)
    w = w_in[0]
    sizes = (GLA_QK_W, GLA_QK_W, GLA_V_W, GLA_V_W, 2 * GLA_LOWRANK, NA_W, NA_W, NA_W)
    offs = np.concatenate([[0], np.cumsum(sizes)])
    gq, gk, gv, gg, glr, nq, nk, nv = [w[:, offs[i]:offs[i + 1]] for i in range(8)]
    lr_pad = jnp.zeros((w.shape[0], LANES - 2 * GLA_LOWRANK), w.dtype)
    wa = jnp.concatenate([gq * (GLA_DK ** -0.5), gk, gg, glr, lr_pad, nq * (NA_DH ** -0.5), nv], axis=1).astype(bf)
    wt = jnp.concatenate([gv, nk], axis=1).T.astype(bf)
    zpad = jnp.zeros((LANES - 2 * GLA_LOWRANK, gw_f.shape[-1]), jnp.float32)
    zlr = jnp.zeros((GLA_LOWRANK, gw_f.shape[-1]), jnp.float32)
    wg_f = jnp.concatenate([gw_f[0], zlr, zpad], axis=0).astype(bf)
    wg_b = jnp.concatenate([zlr, gw_b[0], zpad], axis=0).astype(bf)
    bias = _na_bias_table(na_rpb[0])
    return dict(
        ln_in=(row(ln_in_g), row(ln_in_b)), wa=wa, wt=wt,
        wg_f=wg_f, bg_f=row(gb_f[0]), wg_b=wg_b, bg_b=row(gb_b[0]), ng=row(gla_norm_g[0]),
        tri_f=_cumsum_operator(False), tri_b=_cumsum_operator(True),
        bias=bias, w_out=w_out[0].astype(bf), ln1=(row(ln1_g[0]), row(ln1_b[0])),
        w_up=w_up[0].astype(bf), conv_w=conv_w[0].astype(jnp.float32), conv_b=row(conv_b[0]),
        w_down=w_down[0].astype(bf), ln2=(row(ln2_g[0]), row(ln2_b[0])),
    )


def _trunk(x, p):
    gq, gk, gs, lr, nq, nv, gvt, nkt = _in_proj(x, *p["ln_in"], p["wa"], p["wt"])
    o_bwd = _gla(gq, gk, gvt, lr, p["wg_b"], p["bg_b"], p["tri_b"], reverse=True)
    o_gla = _gla(gq, gk, gvt, lr, p["wg_f"], p["bg_f"], p["tri_f"], reverse=False, gs=gs, ob=o_bwd, ng=p["ng"])
    o_na = _natten(nq, nkt, nv, p["bias"])
    return _tail(x, o_gla, o_na, p["w_out"], *p["ln_in"], *p["ln1"],
                 p["w_up"], p["conv_w"], p["conv_b"], p["w_down"], *p["ln2"])


def kernel(x_prompt, x_sample, ln_in_g, ln_in_b, w_in, gla_gate_w_fwd, gla_gate_b_fwd, gla_gate_w_bwd,
           gla_gate_b_bwd, gla_norm_g, na_rpb, w_out, ln1_g, ln1_b, w_up, conv_w, conv_b, w_down, ln2_g, ln2_b):
    p = _prepare_weights(ln_in_g, ln_in_b, w_in, gla_gate_w_fwd, gla_gate_b_fwd, gla_gate_w_bwd,
                         gla_gate_b_bwd, gla_norm_g, na_rpb, w_out, ln1_g, ln1_b, w_up, conv_w, conv_b,
                         w_down, ln2_g, ln2_b)
    return (_trunk(x_prompt, p), _trunk(x_sample, p))
```
